```python
import jax, jax.numpy as jnp
from jax import lax
import numpy as np

D_MODEL = 1024
BATCH = 1
SEQ = 16384
DEPTH = 1

PLE_DIM = 256
N_HEADS = 8
QK_NOPE = 64
QK_ROPE = 32
V_HEAD = 64
Q_LORA = 384
KV_LORA = 256
POOL_WINDOWS = (2, 4, 8, 16)
POOL_GROUP = 128
POOL_WIDTH = POOL_GROUP * len(POOL_WINDOWS)
N_BRANCH = 2
D_FF = 4 * D_MODEL
ROPE_THETA = 10000.0
EPS = 1e-6
Q_BLOCK = 128
IN_SPLITS = (Q_LORA, KV_LORA, QK_ROPE, POOL_WIDTH, D_MODEL, D_MODEL)
IN_WIDTH = sum(IN_SPLITS)

kernel_name = "hybrid_mla_multiscale_pool_gated_block"


def rmsnorm(x, g):
    xf = x.astype(jnp.float32)
    y = xf * lax.rsqrt(jnp.mean(xf * xf, axis=-1, keepdims=True) + EPS)
    return (y * g.astype(jnp.float32)).astype(x.dtype)


def rope_cos_sin(positions, dim, dtype):
    inv_freq = ROPE_THETA ** (-jnp.arange(0, dim, 2, dtype=jnp.float32) / dim)
    ang = positions.astype(jnp.float32)[..., None] * inv_freq
    return jnp.cos(ang).astype(dtype), jnp.sin(ang).astype(dtype)


def apply_rope(x, cos, sin):
    half = x.shape[-1] // 2
    x1, x2 = x[..., :half], x[..., half:]
    return jnp.concatenate([x1 * cos - x2 * sin, x2 * cos + x1 * sin], axis=-1)


def mla_attention(q_nope, q_rope, k_nope, k_rope, v):
    B, S, H, _ = q_nope.shape
    nb = S // Q_BLOCK
    scale = (QK_NOPE + QK_ROPE) ** -0.5
    kpos = jnp.arange(S)

    def block(args):
        qn, qr, i = args
        s = jnp.einsum('bqhd,bkhd->bhqk', qn, k_nope, preferred_element_type=jnp.float32)
        s = s + jnp.einsum('bqhr,bkr->bhqk', qr, k_rope, preferred_element_type=jnp.float32)
        qpos = i * Q_BLOCK + jnp.arange(Q_BLOCK)
        mask = kpos[None, :] <= qpos[:, None]
        s = jnp.where(mask, s * scale, -jnp.inf)
        pr = jax.nn.softmax(s, axis=-1).astype(v.dtype)
        return jnp.einsum('bhqk,bkhd->bqhd', pr, v)

    qn_b = q_nope.reshape(B, nb, Q_BLOCK, H, QK_NOPE).transpose(1, 0, 2, 3, 4)
    qr_b = q_rope.reshape(B, nb, Q_BLOCK, H, QK_ROPE).transpose(1, 0, 2, 3, 4)
    out = lax.map(block, (qn_b, qr_b, jnp.arange(nb)))
    return out.transpose(1, 0, 2, 3, 4).reshape(B, S, H * V_HEAD)


def multiscale_pool(u, w_pool, pool_scale):
    B, S, _ = u.shape
    uf = u.reshape(B, S, len(POOL_WINDOWS), POOL_GROUP).astype(jnp.float32)
    cs = jnp.cumsum(uf, axis=1)
    t = jnp.arange(S)
    outs = []
    for g, w in enumerate(POOL_WINDOWS):
        c = cs[:, :, g]
        prev = jnp.pad(c, ((0, 0), (w, 0), (0, 0)))[:, :S]
        cnt = jnp.minimum(t + 1, w).astype(jnp.float32)[None, :, None]
        outs.append((c - prev) / cnt - uf[:, :, g])
    d = jnp.stack(outs, axis=2).astype(u.dtype)
    y = jnp.einsum('bsgc,gcd->bsgd', d, w_pool).reshape(B, S, POOL_WIDTH)
    return y * pool_scale


def setup_inputs(seed: int = 0) -> dict:
    key = jax.random.key(seed)
    ks = jax.random.split(key, 24)

    def w(k, shape, fan_in):
        return jax.random.normal(k, shape, jnp.float32) * (fan_in ** -0.5)

    def gain(k, shape):
        return 1.0 + 0.05 * jax.random.normal(k, shape, jnp.float32)

    L = DEPTH
    x = jax.random.normal(ks[0], (BATCH, SEQ, D_MODEL), jnp.float32)
    p = jax.random.normal(ks[1], (DEPTH, BATCH, SEQ, PLE_DIM), jnp.float32)
    offset = jax.random.randint(ks[2], (BATCH, 1), 0, 4096, dtype=jnp.int32)
    positions = offset + jnp.arange(SEQ, dtype=jnp.int32)[None, :]
    return {
        "x": x,
        "p": p,
        "positions": positions,
        "g_pre_mix": gain(ks[3], (L, D_MODEL)),
        "w_in": w(ks[4], (L, D_MODEL, IN_WIDTH), D_MODEL),
        "b_gate": 0.01 * jax.random.normal(ks[5], (L, N_BRANCH * D_MODEL), jnp.float32),
        "g_q": gain(ks[6], (L, Q_LORA)),
        "w_uq": w(ks[7], (L, Q_LORA, N_HEADS * (QK_NOPE + QK_ROPE)), Q_LORA),
        "g_kv": gain(ks[8], (L, KV_LORA)),
        "w_ukv": w(ks[9], (L, KV_LORA, N_HEADS * (QK_NOPE + V_HEAD)), KV_LORA),
        "w_pool": w(ks[10], (L, len(POOL_WINDOWS), POOL_GROUP, POOL_GROUP), POOL_GROUP),
        "pool_scale": gain(ks[11], (L, POOL_WIDTH)),
        "w_branch_attn": w(ks[12], (L, N_HEADS * V_HEAD, D_MODEL), N_HEADS * V_HEAD),
        "w_branch_pool": w(ks[13], (L, POOL_WIDTH, D_MODEL), POOL_WIDTH),
        "w_out": w(ks[14], (L, D_MODEL, D_MODEL), D_MODEL),
        "g_post_mix": gain(ks[15], (L, D_MODEL)),
        "g_pre_mlp": gain(ks[16], (L, D_MODEL)),
        "w_ff1": w(ks[17], (L, D_MODEL, D_FF), D_MODEL),
        "w_ff2": w(ks[18], (L, D_FF, D_MODEL), D_FF),
        "g_post_mlp": gain(ks[19], (L, D_MODEL)),
        "w_ple_proj": w(ks[20], (L, PLE_DIM, D_MODEL), PLE_DIM),
        "w_ple_gate": w(ks[21], (L, D_MODEL, D_MODEL), D_MODEL),
        "g_ple": gain(ks[22], (L, D_MODEL)),
    }


def reference(x, p, positions, g_pre_mix, w_in, b_gate, g_q, w_uq, g_kv, w_ukv,
              w_pool, pool_scale, w_branch_attn, w_branch_pool, w_out, g_post_mix,
              g_pre_mlp, w_ff1, w_ff2, g_post_mlp, w_ple_proj, w_ple_gate, g_ple):
    B, S, _ = x.shape
    cos, sin = rope_cos_sin(positions, QK_ROPE, x.dtype)
    split_idx = list(np.cumsum(IN_SPLITS)[:-1])
    h = x
    for i in range(DEPTH):
        a = rmsnorm(h, g_pre_mix[i])
        proj = a @ w_in[i]
        q_down, kv_down, k_rope, pool_in, gates = (
            *jnp.split(proj, split_idx, axis=-1)[:4],
            proj[..., sum(IN_SPLITS[:4]):])
        gates = jax.nn.sigmoid(gates + b_gate[i])
        gate_attn, gate_pool = gates[..., :D_MODEL], gates[..., D_MODEL:]

        q = (rmsnorm(q_down, g_q[i]) @ w_uq[i]).reshape(B, S, N_HEADS, QK_NOPE + QK_ROPE)
        q_nope = q[..., :QK_NOPE]
        q_rope = apply_rope(q[..., QK_NOPE:], cos[:, :, None, :], sin[:, :, None, :])
        kv = (rmsnorm(kv_down, g_kv[i]) @ w_ukv[i]).reshape(B, S, N_HEADS, QK_NOPE + V_HEAD)
        k_nope, v = kv[..., :QK_NOPE], kv[..., QK_NOPE:]
        k_rope = apply_rope(k_rope, cos, sin)
        attn = mla_attention(q_nope, q_rope, k_nope, k_rope, v)

        pooled = multiscale_pool(pool_in, w_pool[i], pool_scale[i])

        merged = gate_attn * (attn @ w_branch_attn[i]) + gate_pool * (pooled @ w_branch_pool[i])
        h = h + rmsnorm(merged @ w_out[i], g_post_mix[i])

        m = rmsnorm(h, g_pre_mlp[i])
        f = jnp.square(jax.nn.relu(m @ w_ff1[i])) @ w_ff2[i]
        h = h + rmsnorm(f, g_post_mlp[i])

        e = p[i] @ w_ple_proj[i]
        pg = jax.nn.sigmoid(h @ w_ple_gate[i])
        h = h + rmsnorm(pg * e, g_ple[i])
    return h
```

```python
import functools
import math

import jax
import jax.numpy as jnp
from jax import lax
from jax.experimental import pallas as pl
from jax.experimental.pallas import tpu as pltpu

D_MODEL = 1024
PLE_DIM = 256
N_HEADS = 8
QK_NOPE = 64
QK_ROPE = 32
V_HEAD = 64
Q_LORA = 384
KV_LORA = 256
POOL_WINDOWS = (2, 4, 8, 16)
POOL_GROUP = 128
POOL_WIDTH = POOL_GROUP * len(POOL_WINDOWS)
D_FF = 4 * D_MODEL
ROPE_THETA = 10000.0
EPS = 1e-6

LANES = 128
HEAD_PAD = LANES
ROPE_LANE0 = QK_NOPE
ROPE_HALF = QK_ROPE // 2
POOL_HALO = max(POOL_WINDOWS)
MASK_VALUE = -1e30

QKV_ROWS = 512
ATT_BLOCK = 512
POST_ROWS = 512
FF_CHUNK = 1024
VMEM_LIMIT = 60 * 1024 * 1024

Q_SCALE = (QK_NOPE + QK_ROPE) ** -0.5 * math.log2(math.e)


def _rms(x, g):
    y = x * lax.rsqrt(jnp.mean(x * x, axis=-1, keepdims=True) + EPS)
    return y * g


def _dot(a, b):
    return jnp.dot(a, b, preferred_element_type=jnp.float32)


def _qkv_kernel(x_ref, pos_ref, invf_ref, g_ref, win_ref, gq_ref, wuq_ref, gkv_ref,
                wuk_ref, wuv_ref, q_ref, k_ref, v_ref):
    bf = jnp.bfloat16
    a = _rms(x_ref[...], g_ref[...]).astype(bf)
    proj = _dot(a, win_ref[...])
    q_down = proj[:, :Q_LORA]
    kv_down = proj[:, Q_LORA:Q_LORA + KV_LORA]
    k_rope = proj[:, Q_LORA + KV_LORA:]

    q_all = _dot(_rms(q_down, gq_ref[...]).astype(bf), wuq_ref[...])
    kv_n = _rms(kv_down, gkv_ref[...]).astype(bf)
    k_all = _dot(kv_n, wuk_ref[...])
    v_all = _dot(kv_n, wuv_ref[...])

    ang = pos_ref[...].astype(jnp.float32) * invf_ref[...]
    lane = lax.broadcasted_iota(jnp.int32, ang.shape, 1)
    in_x1 = (lane >= ROPE_LANE0) & (lane < ROPE_LANE0 + ROPE_HALF)
    in_x2 = (lane >= ROPE_LANE0 + ROPE_HALF) & (lane < ROPE_LANE0 + QK_ROPE)
    cos = jnp.where(in_x1 | in_x2, jnp.cos(ang), 1.0)
    sin = jnp.sin(ang)
    sin_from_x2 = jnp.where(in_x1, -sin, 0.0)
    sin_from_x1 = jnp.where(in_x2, sin, 0.0)

    def rope(z):
        up = pltpu.roll(z, LANES - ROPE_HALF, 1)
        down = pltpu.roll(z, ROPE_HALF, 1)
        return z * cos + up * sin_from_x2 + down * sin_from_x1

    k_rope = rope(k_rope)
    lane_row = lax.broadcasted_iota(jnp.int32, (1, HEAD_PAD), 1)
    for h in range(N_HEADS):
        sl = slice(h * HEAD_PAD, (h + 1) * HEAD_PAD)
        q_ref[:, sl] = (rope(q_all[:, sl]) * Q_SCALE).astype(bf)
        k_ref[:, sl] = (k_all[:, sl] + k_rope).astype(bf)
        ones_lane = V_HEAD if h % 2 == 0 else 0
        v_ref[:, sl] = (v_all[:, sl] + (lane_row == ones_lane).astype(jnp.float32)).astype(bf)


def _qkv_call(x, pos, invf, g, win, gq, wuq, gkv, wuk, wuv):
    s = x.shape[0]
    tm = QKV_ROWS
    row = lambda i: (i, 0)
    fixed = lambda i: (0, 0)
    full = lambda arr: pl.BlockSpec(arr.shape, fixed)
    out = jax.ShapeDtypeStruct((s, N_HEADS * HEAD_PAD), jnp.bfloat16)
    return pl.pallas_call(
        _qkv_kernel,
        grid=(s // tm,),
        in_specs=[pl.BlockSpec((tm, D_MODEL), row), pl.BlockSpec((tm, 1), row), full(invf),
                  full(g), full(win), full(gq), full(wuq), full(gkv), full(wuk), full(wuv)],
        out_specs=[pl.BlockSpec((tm, N_HEADS * HEAD_PAD), row)] * 3,
        out_shape=[out, out, out],
        compiler_params=pltpu.CompilerParams(
            dimension_semantics=("arbitrary",), vmem_limit_bytes=VMEM_LIMIT),
        name="qkv_proj",
    )(x, pos, invf, g, win, gq, wuq, gkv, wuk, wuv)


def _attn_kernel(q_ref, k_ref, v_ref, o_ref, m_ref, acc_ref):
    tq = q_ref.shape[0]
    tk = tq
    i = pl.program_id(1)
    m_ref[...] = jnp.full(m_ref.shape, MASK_VALUE, jnp.float32)
    acc_ref[...] = jnp.zeros(acc_ref.shape, jnp.float32)

    def step(j, masked):
        rows = pl.ds(pl.multiple_of(j * tk, tk), tk)
        for hh in range(2):
            sl = slice(hh * HEAD_PAD, (hh + 1) * HEAD_PAD)
            s = lax.dot_general(q_ref[:, sl], k_ref[rows, sl], (((1,), (1,)), ((), ())),
                                preferred_element_type=jnp.float32)
            if masked:
                r = lax.broadcasted_iota(jnp.int32, s.shape, 0)
                c = lax.broadcasted_iota(jnp.int32, s.shape, 1)
                s = jnp.where(c <= r, s, MASK_VALUE)
            m_prev = m_ref[hh]
            m_new = jnp.maximum(m_prev, jnp.max(s, axis=1, keepdims=True))
            alpha = jnp.exp2(m_prev - m_new)
            p = jnp.exp2(s - pltpu.repeat(m_new, tk // LANES, axis=1)).astype(jnp.bfloat16)
            acc_ref[hh] = alpha * acc_ref[hh] + _dot(p, v_ref[rows, sl])
            m_ref[hh] = m_new

    def body(j, carry):
        step(j, False)
        return carry

    lax.fori_loop(0, i, body, 0)
    step(i, True)

    acc0 = acc_ref[0]
    acc1 = acc_ref[1]
    out0 = acc0 / acc0[:, V_HEAD:V_HEAD + 1]
    out1 = acc1 / acc1[:, 0:1]
    lane = lax.broadcasted_iota(jnp.int32, out0.shape, 1)
    o_ref[...] = jnp.where(lane < V_HEAD, out0, out1).astype(o_ref.dtype)


def _attn_call(q, k, v):
    s = q.shape[0]
    tq = ATT_BLOCK
    pair = 2 * HEAD_PAD
    return pl.pallas_call(
        _attn_kernel,
        grid=(N_HEADS // 2, s // tq),
        in_specs=[pl.BlockSpec((tq, pair), lambda hp, i: (i, hp)),
                  pl.BlockSpec((s, pair), lambda hp, i: (0, hp)),
                  pl.BlockSpec((s, pair), lambda hp, i: (0, hp))],
        out_specs=pl.BlockSpec((tq, 2 * V_HEAD), lambda hp, i: (i, hp)),
        out_shape=jax.ShapeDtypeStruct((s, N_HEADS * V_HEAD), jnp.bfloat16),
        scratch_shapes=[pltpu.VMEM((2, tq, HEAD_PAD), jnp.float32),
                        pltpu.VMEM((2, tq, HEAD_PAD), jnp.float32)],
        compiler_params=pltpu.CompilerParams(
            dimension_semantics=("arbitrary", "arbitrary"), vmem_limit_bytes=VMEM_LIMIT),
        name="mla_attention",
    )(q, k, v)


def _post_kernel(x_ref, halo_ref, attn_ref, p_ref, g_pre_ref, wpool_in_ref, wgate_ref, bgate_ref,
                 wpool_ref, pscale_ref, wba_ref, wbp_ref, wout_ref, g_post_ref, g_mlp_ref,
                 wff1_ref, wff2_ref, g_pmlp_ref, wpe_ref, wpg_ref, g_ple_ref, o_ref, ext_ref):
    bf = jnp.bfloat16
    tm = x_ref.shape[0]
    i = pl.program_id(0)
    x = x_ref[...]
    a = _rms(x, g_pre_ref[...]).astype(bf)

    a_halo = _rms(halo_ref[...], g_pre_ref[...]).astype(bf)
    u_halo = _dot(a_halo, wpool_in_ref[...])
    u = _dot(a, wpool_in_ref[...])
    ext_ref[0:POOL_HALO, :] = jnp.where(i > 0, u_halo, 0.0)
    ext_ref[POOL_HALO:, :] = u
    t = i * tm + lax.broadcasted_iota(jnp.int32, (tm, 1), 0)
    pooled = []
    for g, w in enumerate(POOL_WINDOWS):
        cols = slice(g * POOL_GROUP, (g + 1) * POOL_GROUP)
        wsum = ext_ref[POOL_HALO:POOL_HALO + tm, cols]
        for back in range(1, w):
            wsum = wsum + ext_ref[POOL_HALO - back:POOL_HALO - back + tm, cols]
        cnt = jnp.minimum(t + 1, w).astype(jnp.float32)
        d = wsum / cnt - u[:, cols]
        pooled.append(_dot(d.astype(bf), wpool_ref[g]))
    pooled = jnp.concatenate(pooled, axis=1) * pscale_ref[...]

    gates = jax.nn.sigmoid(_dot(a, wgate_ref[...]) + bgate_ref[...])
    merged = (gates[:, :D_MODEL] * _dot(attn_ref[...], wba_ref[...])
              + gates[:, D_MODEL:] * _dot(pooled.astype(bf), wbp_ref[...]))
    h = x + _rms(_dot(merged.astype(bf), wout_ref[...]), g_post_ref[...])

    m = _rms(h, g_mlp_ref[...]).astype(bf)
    f = jnp.zeros((tm, D_MODEL), jnp.float32)
    for c in range(D_FF // FF_CHUNK):
        cols = slice(c * FF_CHUNK, (c + 1) * FF_CHUNK)
        hid = jnp.square(jnp.maximum(_dot(m, wff1_ref[:, cols]), 0.0))
        f = f + _dot(hid.astype(bf), wff2_ref[cols, :])
    h = h + _rms(f, g_pmlp_ref[...])

    e = _dot(p_ref[...].astype(bf), wpe_ref[...])
    pg = jax.nn.sigmoid(_dot(h.astype(bf), wpg_ref[...]))
    o_ref[...] = h + _rms(pg * e, g_ple_ref[...])


def _post_call(x, attn, p, g_pre, wpool_in, wgate, bgate, wpool, pscale, wba, wbp, wout,
               g_post, g_mlp, wff1, wff2, g_pmlp, wpe, wpg, g_ple):
    s = x.shape[0]
    tm = POST_ROWS
    row = lambda i: (i, 0)
    halo_blocks = tm // POOL_HALO

    def const(arr):
        zeros = (0,) * arr.ndim
        return pl.BlockSpec(arr.shape, lambda i: zeros, pipeline_mode=pl.Buffered(1))

    weights = (g_pre, wpool_in, wgate, bgate, wpool, pscale, wba, wbp, wout, g_post, g_mlp,
               wff1, wff2, g_pmlp, wpe, wpg, g_ple)
    return pl.pallas_call(
        _post_kernel,
        grid=(s // tm,),
        in_specs=[pl.BlockSpec((tm, D_MODEL), row),
                  pl.BlockSpec((POOL_HALO, D_MODEL),
                               lambda i: (jnp.maximum(i * halo_blocks - 1, 0), 0)),
                  pl.BlockSpec((tm, N_HEADS * V_HEAD), row),
                  pl.BlockSpec((tm, PLE_DIM), row)] + [const(w) for w in weights],
        out_specs=pl.BlockSpec((tm, D_MODEL), row),
        out_shape=jax.ShapeDtypeStruct((s, D_MODEL), jnp.float32),
        scratch_shapes=[pltpu.VMEM((tm + POOL_HALO, POOL_WIDTH), jnp.float32)],
        compiler_params=pltpu.CompilerParams(
            dimension_semantics=("arbitrary",), vmem_limit_bytes=VMEM_LIMIT),
        name="post_attention",
    )(x, x, attn, p, *weights)


def _pad_heads(w, per_head, lane0=0):
    kdim = w.shape[0]
    w = w.reshape(kdim, N_HEADS, per_head)
    w = jnp.pad(w, ((0, 0), (0, 0), (lane0, HEAD_PAD - per_head - lane0)))
    return w.reshape(kdim, N_HEADS * HEAD_PAD)


def _layer(h, p, pos, invf, g_pre_mix, w_in, b_gate, g_q, w_uq, g_kv, w_ukv, w_pool, pool_scale,
           w_branch_attn, w_branch_pool, w_out, g_post_mix, g_pre_mlp, w_ff1, w_ff2, g_post_mlp,
           w_ple_proj, w_ple_gate, g_ple):
    bf = jnp.bfloat16
    row = lambda v: v.reshape(1, -1)
    o_kv = Q_LORA
    o_kr = o_kv + KV_LORA
    o_pool = o_kr + QK_ROPE
    o_gate = o_pool + POOL_WIDTH

    w_kr = jnp.pad(w_in[:, o_kr:o_pool], ((0, 0), (ROPE_LANE0, HEAD_PAD - ROPE_LANE0 - QK_ROPE)))
    w_in_qkv = jnp.concatenate([w_in[:, :o_kr], w_kr], axis=1).astype(bf)
    w_uq_p = _pad_heads(w_uq, QK_NOPE + QK_ROPE).astype(bf)
    w_ukv_h = w_ukv.reshape(KV_LORA, N_HEADS, QK_NOPE + V_HEAD)
    w_uk_p = _pad_heads(w_ukv_h[:, :, :QK_NOPE].reshape(KV_LORA, -1), QK_NOPE).astype(bf)
    w_uv = w_ukv_h[:, :, QK_NOPE:]
    w_uv_p = jnp.stack(
        [jnp.pad(w_uv[:, hd], ((0, 0), (0, V_HEAD) if hd % 2 == 0 else (V_HEAD, 0)))
         for hd in range(N_HEADS)], axis=1).reshape(KV_LORA, N_HEADS * HEAD_PAD).astype(bf)

    q, k, v = _qkv_call(h, pos, invf, row(g_pre_mix), w_in_qkv, row(g_q), w_uq_p, row(g_kv),
                        w_uk_p, w_uv_p)
    attn = _attn_call(q, k, v)
    return _post_call(
        h, attn, p, row(g_pre_mix), w_in[:, o_pool:o_gate].astype(bf), w_in[:, o_gate:].astype(bf),
        row(b_gate), w_pool.astype(bf), row(pool_scale), w_branch_attn.astype(bf),
        w_branch_pool.astype(bf), w_out.astype(bf), row(g_post_mix), row(g_pre_mlp),
        w_ff1.astype(bf), w_ff2.astype(bf), row(g_post_mlp), w_ple_proj.astype(bf),
        w_ple_gate.astype(bf), row(g_ple))


@jax.jit
def kernel(x, p, positions, g_pre_mix, w_in, b_gate, g_q, w_uq, g_kv, w_ukv, w_pool, pool_scale,
           w_branch_attn, w_branch_pool, w_out, g_post_mix, g_pre_mlp, w_ff1, w_ff2, g_post_mlp,
           w_ple_proj, w_ple_gate, g_ple):
    batch, seq, _ = x.shape
    depth = w_in.shape[0]
    inv_freq = ROPE_THETA ** (-jnp.arange(0, QK_ROPE, 2, dtype=jnp.float32) / QK_ROPE)
    invf = jnp.zeros((1, HEAD_PAD), jnp.float32)
    invf = invf.at[0, ROPE_LANE0:ROPE_LANE0 + ROPE_HALF].set(inv_freq)
    invf = invf.at[0, ROPE_LANE0 + ROPE_HALF:ROPE_LANE0 + QK_ROPE].set(inv_freq)
    outs = []
    for b in range(batch):
        h = x[b]
        pos = positions[b].reshape(seq, 1)
        for l in range(depth):
            h = _layer(h, p[l, b], pos, invf, g_pre_mix[l], w_in[l], b_gate[l], g_q[l], w_uq[l],
                       g_kv[l], w_ukv[l], w_pool[l], pool_scale[l], w_branch_attn[l],
                       w_branch_pool[l], w_out[l], g_post_mix[l], g_pre_mlp[l], w_ff1[l], w_ff2[l],
                       g_post_mlp[l], w_ple_proj[l], w_ple_gate[l], g_ple[l])
        outs.append(h)
    return jnp.stack(outs, axis=0)
```

```python
import functools
import math

import jax
import jax.numpy as jnp
from jax import lax
from jax.experimental import pallas as pl
from jax.experimental.pallas import tpu as pltpu

D_MODEL = 1024
PLE_DIM = 256
N_HEADS = 8
QK_NOPE = 64
QK_ROPE = 32
V_HEAD = 64
Q_LORA = 384
KV_LORA = 256
POOL_WINDOWS = (2, 4, 8, 16)
POOL_GROUP = 128
POOL_WIDTH = POOL_GROUP * len(POOL_WINDOWS)
D_FF = 4 * D_MODEL
ROPE_THETA = 10000.0
EPS = 1e-6

LANES = 128
HEAD_PAD = LANES
ROPE_LANE0 = QK_NOPE
ROPE_HALF = QK_ROPE // 2
POOL_HALO = max(POOL_WINDOWS)
MASK_VALUE = -1e30

QKV_ROWS = 512
ATT_Q = 1024
ATT_K = 1024
ATT_DIAG = 512
POST_ROWS = 512
FF_CHUNK = 1024
VMEM_LIMIT = 60 * 1024 * 1024

Q_SCALE = (QK_NOPE + QK_ROPE) ** -0.5 * math.log2(math.e)


def _rms(x, g):
    y = x * lax.rsqrt(jnp.mean(x * x, axis=-1, keepdims=True) + EPS)
    return y * g


def _dot(a, b):
    return jnp.dot(a, b, preferred_element_type=jnp.float32)


def _qkv_kernel(x_ref, pos_ref, invf_ref, g_ref, win_ref, gq_ref, wuq_ref, gkv_ref,
                wuk_ref, wuv_ref, q_ref, k_ref, v_ref):
    bf = jnp.bfloat16
    a = _rms(x_ref[...], g_ref[...]).astype(bf)
    proj = _dot(a, win_ref[...])
    q_down = proj[:, :Q_LORA]
    kv_down = proj[:, Q_LORA:Q_LORA + KV_LORA]
    k_rope = proj[:, Q_LORA + KV_LORA:]

    q_all = _dot(_rms(q_down, gq_ref[...]).astype(bf), wuq_ref[...])
    kv_n = _rms(kv_down, gkv_ref[...]).astype(bf)
    k_all = _dot(kv_n, wuk_ref[...])
    v_all = _dot(kv_n, wuv_ref[...])

    ang = pos_ref[...].astype(jnp.float32) * invf_ref[...]
    lane = lax.broadcasted_iota(jnp.int32, ang.shape, 1)
    in_x1 = (lane >= ROPE_LANE0) & (lane < ROPE_LANE0 + ROPE_HALF)
    in_x2 = (lane >= ROPE_LANE0 + ROPE_HALF) & (lane < ROPE_LANE0 + QK_ROPE)
    cos = jnp.where(in_x1 | in_x2, jnp.cos(ang), 1.0)
    sin = jnp.sin(ang)
    sin_from_x2 = jnp.where(in_x1, -sin, 0.0)
    sin_from_x1 = jnp.where(in_x2, sin, 0.0)

    def rope(z):
        up = pltpu.roll(z, LANES - ROPE_HALF, 1)
        down = pltpu.roll(z, ROPE_HALF, 1)
        return z * cos + up * sin_from_x2 + down * sin_from_x1

    k_rope = rope(k_rope)
    lane_row = lax.broadcasted_iota(jnp.int32, (1, HEAD_PAD), 1)
    for h in range(N_HEADS):
        sl = slice(h * HEAD_PAD, (h + 1) * HEAD_PAD)
        q_ref[:, sl] = (rope(q_all[:, sl]) * Q_SCALE).astype(bf)
        k_ref[:, sl] = (k_all[:, sl] + k_rope).astype(bf)
        ones_lane = V_HEAD if h % 2 == 0 else 0
        v_ref[:, sl] = (v_all[:, sl] + (lane_row == ones_lane).astype(jnp.float32)).astype(bf)


def _qkv_call(x, pos, invf, g, win, gq, wuq, gkv, wuk, wuv):
    s = x.shape[0]
    tm = QKV_ROWS
    row = lambda i: (i, 0)
    fixed = lambda i: (0, 0)
    full = lambda arr: pl.BlockSpec(arr.shape, fixed)
    out = jax.ShapeDtypeStruct((s, N_HEADS * HEAD_PAD), jnp.bfloat16)
    return pl.pallas_call(
        _qkv_kernel,
        grid=(s // tm,),
        in_specs=[pl.BlockSpec((tm, D_MODEL), row), pl.BlockSpec((tm, 1), row), full(invf),
                  full(g), full(win), full(gq), full(wuq), full(gkv), full(wuk), full(wuv)],
        out_specs=[pl.BlockSpec((tm, N_HEADS * HEAD_PAD), row)] * 3,
        out_shape=[out, out, out],
        compiler_params=pltpu.CompilerParams(
            dimension_semantics=("arbitrary",), vmem_limit_bytes=VMEM_LIMIT),
        name="qkv_proj",
    )(x, pos, invf, g, win, gq, wuq, gkv, wuk, wuv)


def _attn_kernel(q_ref, k_ref, v_ref, o_ref, m_ref, acc_ref, *, tk, td):
    tq = q_ref.shape[0]
    i = pl.program_id(1)
    m_ref[...] = jnp.full(m_ref.shape, MASK_VALUE, jnp.float32)
    acc_ref[...] = jnp.zeros(acc_ref.shape, jnp.float32)

    def step(kv_start, width, row_lo, n_rows, diag_shift=None):
        kv = pl.ds(kv_start, width)
        qr = slice(row_lo, row_lo + n_rows)
        for hh in range(2):
            sl = slice(hh * HEAD_PAD, (hh + 1) * HEAD_PAD)
            s = lax.dot_general(q_ref[qr, sl], k_ref[kv, sl], (((1,), (1,)), ((), ())),
                                preferred_element_type=jnp.float32)
            if diag_shift is not None:
                r = lax.broadcasted_iota(jnp.int32, s.shape, 0)
                c = lax.broadcasted_iota(jnp.int32, s.shape, 1)
                s = jnp.where(c <= r + diag_shift, s, MASK_VALUE)
            m_prev = m_ref[hh, qr]
            m_new = jnp.maximum(m_prev, jnp.max(s, axis=1, keepdims=True))
            alpha = jnp.exp2(m_prev - m_new)
            p = jnp.exp2(s - pltpu.repeat(m_new, width // LANES, axis=1)).astype(jnp.bfloat16)
            acc_ref[hh, qr] = alpha * acc_ref[hh, qr] + _dot(p, v_ref[kv, sl])
            m_ref[hh, qr] = m_new

    def body(j, carry):
        step(pl.multiple_of(j * tk, tk), tk, 0, tq)
        return carry

    q_start = i * tq
    n_main = q_start // tk
    lax.fori_loop(0, n_main, body, 0)
    if tk > tq:
        for d in range(tk // tq - 1):
            @pl.when(q_start - n_main * tk > d * tq)
            def _():
                step(pl.multiple_of(n_main * tk + d * tq, tq), tq, 0, tq)
    for d in range(tq // td):
        step(pl.multiple_of(q_start, tq), (d + 1) * td, d * td, td, diag_shift=d * td)

    acc0 = acc_ref[0]
    acc1 = acc_ref[1]
    out0 = acc0 / acc0[:, V_HEAD:V_HEAD + 1]
    out1 = acc1 / acc1[:, 0:1]
    lane = lax.broadcasted_iota(jnp.int32, out0.shape, 1)
    o_ref[...] = jnp.where(lane < V_HEAD, out0, out1).astype(o_ref.dtype)


def _attn_call(q, k, v):
    s = q.shape[0]
    tq = ATT_Q
    pair = 2 * HEAD_PAD
    return pl.pallas_call(
        functools.partial(_attn_kernel, tk=ATT_K, td=ATT_DIAG),
        grid=(N_HEADS // 2, s // tq),
        in_specs=[pl.BlockSpec((tq, pair), lambda hp, i: (i, hp)),
                  pl.BlockSpec((s, pair), lambda hp, i: (0, hp)),
                  pl.BlockSpec((s, pair), lambda hp, i: (0, hp))],
        out_specs=pl.BlockSpec((tq, 2 * V_HEAD), lambda hp, i: (i, hp)),
        out_shape=jax.ShapeDtypeStruct((s, N_HEADS * V_HEAD), jnp.bfloat16),
        scratch_shapes=[pltpu.VMEM((2, tq, HEAD_PAD), jnp.float32),
                        pltpu.VMEM((2, tq, HEAD_PAD), jnp.float32)],
        compiler_params=pltpu.CompilerParams(
            dimension_semantics=("arbitrary", "arbitrary"), vmem_limit_bytes=VMEM_LIMIT),
        name="mla_attention",
    )(q, k, v)


def _post_kernel(x_ref, halo_ref, attn_ref, p_ref, g_pre_ref, wpool_in_ref, wgate_ref, bgate_ref,
                 wpool_ref, pscale_ref, wba_ref, wbp_ref, wout_ref, g_post_ref, g_mlp_ref,
                 wff1_ref, wff2_ref, g_pmlp_ref, wpe_ref, wpg_ref, g_ple_ref, o_ref, ext_ref):
    bf = jnp.bfloat16
    tm = x_ref.shape[0]
    i = pl.program_id(0)
    x = x_ref[...]
    a = _rms(x, g_pre_ref[...]).astype(bf)

    a_halo = _rms(halo_ref[...], g_pre_ref[...]).astype(bf)
    u_halo = _dot(a_halo, wpool_in_ref[...])
    u = _dot(a, wpool_in_ref[...])
    ext_ref[0:POOL_HALO, :] = jnp.where(i > 0, u_halo, 0.0)
    ext_ref[POOL_HALO:, :] = u
    t = i * tm + lax.broadcasted_iota(jnp.int32, (tm, 1), 0)
    pooled = []
    for g, w in enumerate(POOL_WINDOWS):
        cols = slice(g * POOL_GROUP, (g + 1) * POOL_GROUP)
        wsum = ext_ref[POOL_HALO:POOL_HALO + tm, cols]
        for back in range(1, w):
            wsum = wsum + ext_ref[POOL_HALO - back:POOL_HALO - back + tm, cols]
        cnt = jnp.minimum(t + 1, w).astype(jnp.float32)
        d = wsum / cnt - u[:, cols]
        pooled.append(_dot(d.astype(bf), wpool_ref[g]))
    pooled = jnp.concatenate(pooled, axis=1) * pscale_ref[...]

    gates = jax.nn.sigmoid(_dot(a, wgate_ref[...]) + bgate_ref[...])
    merged = (gates[:, :D_MODEL] * _dot(attn_ref[...], wba_ref[...])
              + gates[:, D_MODEL:] * _dot(pooled.astype(bf), wbp_ref[...]))
    h = x + _rms(_dot(merged.astype(bf), wout_ref[...]), g_post_ref[...])

    m = _rms(h, g_mlp_ref[...]).astype(bf)
    f = jnp.zeros((tm, D_MODEL), jnp.float32)
    for c in range(D_FF // FF_CHUNK):
        cols = slice(c * FF_CHUNK, (c + 1) * FF_CHUNK)
        hid = jnp.square(jnp.maximum(_dot(m, wff1_ref[:, cols]), 0.0))
        f = f + _dot(hid.astype(bf), wff2_ref[cols, :])
    h = h + _rms(f, g_pmlp_ref[...])

    e = _dot(p_ref[...].astype(bf), wpe_ref[...])
    pg = jax.nn.sigmoid(_dot(h.astype(bf), wpg_ref[...]))
    o_ref[...] = h + _rms(pg * e, g_ple_ref[...])


def _post_call(x, attn, p, g_pre, wpool_in, wgate, bgate, wpool, pscale, wba, wbp, wout,
               g_post, g_mlp, wff1, wff2, g_pmlp, wpe, wpg, g_ple):
    s = x.shape[0]
    tm = POST_ROWS
    row = lambda i: (i, 0)
    halo_blocks = tm // POOL_HALO

    def const(arr):
        zeros = (0,) * arr.ndim
        return pl.BlockSpec(arr.shape, lambda i: zeros, pipeline_mode=pl.Buffered(1))

    weights = (g_pre, wpool_in, wgate, bgate, wpool, pscale, wba, wbp, wout, g_post, g_mlp,
               wff1, wff2, g_pmlp, wpe, wpg, g_ple)
    return pl.pallas_call(
        _post_kernel,
        grid=(s // tm,),
        in_specs=[pl.BlockSpec((tm, D_MODEL), row),
                  pl.BlockSpec((POOL_HALO, D_MODEL),
                               lambda i: (jnp.maximum(i * halo_blocks - 1, 0), 0)),
                  pl.BlockSpec((tm, N_HEADS * V_HEAD), row),
                  pl.BlockSpec((tm, PLE_DIM), row)] + [const(w) for w in weights],
        out_specs=pl.BlockSpec((tm, D_MODEL), row),
        out_shape=jax.ShapeDtypeStruct((s, D_MODEL), jnp.float32),
        scratch_shapes=[pltpu.VMEM((tm + POOL_HALO, POOL_WIDTH), jnp.float32)],
        compiler_params=pltpu.CompilerParams(
            dimension_semantics=("arbitrary",), vmem_limit_bytes=VMEM_LIMIT),
        name="post_attention",
    )(x, x, attn, p, *weights)


def _pad_heads(w, per_head, lane0=0):
    kdim = w.shape[0]
    w = w.reshape(kdim, N_HEADS, per_head)
    w = jnp.pad(w, ((0, 0), (0, 0), (lane0, HEAD_PAD - per_head - lane0)))
    return w.reshape(kdim, N_HEADS * HEAD_PAD)


def _layer(h, p, pos, invf, g_pre_mix, w_in, b_gate, g_q, w_uq, g_kv, w_ukv, w_pool, pool_scale,
           w_branch_attn, w_branch_pool, w_out, g_post_mix, g_pre_mlp, w_ff1, w_ff2, g_post_mlp,
           w_ple_proj, w_ple_gate, g_ple):
    bf = jnp.bfloat16
    row = lambda v: v.reshape(1, -1)
    o_kv = Q_LORA
    o_kr = o_kv + KV_LORA
    o_pool = o_kr + QK_ROPE
    o_gate = o_pool + POOL_WIDTH

    w_kr = jnp.pad(w_in[:, o_kr:o_pool], ((0, 0), (ROPE_LANE0, HEAD_PAD - ROPE_LANE0 - QK_ROPE)))
    w_in_qkv = jnp.concatenate([w_in[:, :o_kr], w_kr], axis=1).astype(bf)
    w_uq_p = _pad_heads(w_uq, QK_NOPE + QK_ROPE).astype(bf)
    w_ukv_h = w_ukv.reshape(KV_LORA, N_HEADS, QK_NOPE + V_HEAD)
    w_uk_p = _pad_heads(w_ukv_h[:, :, :QK_NOPE].reshape(KV_LORA, -1), QK_NOPE).astype(bf)
    w_uv = w_ukv_h[:, :, QK_NOPE:]
    w_uv_p = jnp.stack(
        [jnp.pad(w_uv[:, hd], ((0, 0), (0, V_HEAD) if hd % 2 == 0 else (V_HEAD, 0)))
         for hd in range(N_HEADS)], axis=1).reshape(KV_LORA, N_HEADS * HEAD_PAD).astype(bf)

    q, k, v = _qkv_call(h, pos, invf, row(g_pre_mix), w_in_qkv, row(g_q), w_uq_p, row(g_kv),
                        w_uk_p, w_uv_p)
    attn = _attn_call(q, k, v)
    return _post_call(
        h, attn, p, row(g_pre_mix), w_in[:, o_pool:o_gate].astype(bf), w_in[:, o_gate:].astype(bf),
        row(b_gate), w_pool.astype(bf), row(pool_scale), w_branch_attn.astype(bf),
        w_branch_pool.astype(bf), w_out.astype(bf), row(g_post_mix), row(g_pre_mlp),
        w_ff1.astype(bf), w_ff2.astype(bf), row(g_post_mlp), w_ple_proj.astype(bf),
        w_ple_gate.astype(bf), row(g_ple))


@jax.jit
def kernel(x, p, positions, g_pre_mix, w_in, b_gate, g_q, w_uq, g_kv, w_ukv, w_pool, pool_scale,
           w_branch_attn, w_branch_pool, w_out, g_post_mix, g_pre_mlp, w_ff1, w_ff2, g_post_mlp,
           w_ple_proj, w_ple_gate, g_ple):
    batch, seq, _ = x.shape
    depth = w_in.shape[0]
    inv_freq = ROPE_THETA ** (-jnp.arange(0, QK_ROPE, 2, dtype=jnp.float32) / QK_ROPE)
    invf = jnp.zeros((1, HEAD_PAD), jnp.float32)
    invf = invf.at[0, ROPE_LANE0:ROPE_LANE0 + ROPE_HALF].set(inv_freq)
    invf = invf.at[0, ROPE_LANE0 + ROPE_HALF:ROPE_LANE0 + QK_ROPE].set(inv_freq)
    outs = []
    for b in range(batch):
        h = x[b]
        pos = positions[b].reshape(seq, 1)
        for l in range(depth):
            h = _layer(h, p[l, b], pos, invf, g_pre_mix[l], w_in[l], b_gate[l], g_q[l], w_uq[l],
                       g_kv[l], w_ukv[l], w_pool[l], pool_scale[l], w_branch_attn[l],
                       w_branch_pool[l], w_out[l], g_post_mix[l], g_pre_mlp[l], w_ff1[l], w_ff2[l],
                       g_post_mlp[l], w_ple_proj[l], w_ple_gate[l], g_ple[l])
        outs.append(h)
    return jnp.stack(outs, axis=0)
```

```python
import functools
import math
from typing import Any, NamedTuple, Optional

import jax
import jax.numpy as jnp
from jax import lax
from jax.experimental import pallas as pl
from jax.experimental.pallas import tpu as pltpu

D_MODEL = 1024
PLE_DIM = 256
N_HEADS = 8
QK_NOPE = 64
QK_ROPE = 32
V_HEAD = 64
Q_LORA = 384
KV_LORA = 256
POOL_WINDOWS = (2, 4, 8, 16)
POOL_GROUP = 128
POOL_WIDTH = POOL_GROUP * len(POOL_WINDOWS)
D_FF = 4 * D_MODEL
ROPE_THETA = 10000.0
EPS = 1e-6

LANES = 128
HEAD_PAD = LANES
ROPE_LANE0 = QK_NOPE
ROPE_HALF = QK_ROPE // 2
POOL_HALO = max(POOL_WINDOWS)
MASK_VALUE = -1e30

V_ROWS = 80
V_CHUNK = 512
ATT_Q = 2048
ATT_K = 1024
ATT_DIAG = 256
ATT_TILE = 256
MAX_CHAINS = 8
S_BUFS = 4
POST_ROWS = 512
FF_CHUNK = 1024
VMEM_LIMIT = 60 * 1024 * 1024

Q_SCALE = (QK_NOPE + QK_ROPE) ** -0.5 * math.log2(math.e)


def _rms(x, g):
    y = x * lax.rsqrt(jnp.mean(x * x, axis=-1, keepdims=True) + EPS)
    return y * g


def _dot(a, b):
    return jnp.dot(a, b, preferred_element_type=jnp.float32)


def _qkv_kernel(x_ref, pos_ref, invf_ref, g_ref, win_ref, gq_ref, wuq_ref, gkv_ref,
                wuk_ref, wuvt_ref, q_ref, k_ref, vt_ref):
    bf = jnp.bfloat16
    a = _rms(x_ref[...], g_ref[...]).astype(bf)
    proj = _dot(a, win_ref[...])
    q_down = proj[:, :Q_LORA]
    kv_down = proj[:, Q_LORA:Q_LORA + KV_LORA]
    k_rope = proj[:, Q_LORA + KV_LORA:]

    q_all = _dot(_rms(q_down, gq_ref[...]).astype(bf), wuq_ref[...])
    kv_n = _rms(kv_down, gkv_ref[...]).astype(bf)
    k_all = _dot(kv_n, wuk_ref[...])
    v_t = lax.dot_general(wuvt_ref[...], kv_n, (((1,), (1,)), ((), ())),
                          preferred_element_type=jnp.float32)
    v_row = lax.broadcasted_iota(jnp.int32, v_t.shape, 0) % V_ROWS
    vt_ref[0] = jnp.where(v_row == V_HEAD, 1.0, v_t).astype(bf)

    ang = pos_ref[...].astype(jnp.float32) * invf_ref[...]
    lane = lax.broadcasted_iota(jnp.int32, ang.shape, 1)
    in_x1 = (lane >= ROPE_LANE0) & (lane < ROPE_LANE0 + ROPE_HALF)
    in_x2 = (lane >= ROPE_LANE0 + ROPE_HALF) & (lane < ROPE_LANE0 + QK_ROPE)
    cos = jnp.where(in_x1 | in_x2, jnp.cos(ang), 1.0)
    sin = jnp.sin(ang)
    sin_from_x2 = jnp.where(in_x1, -sin, 0.0)
    sin_from_x1 = jnp.where(in_x2, sin, 0.0)

    def rope(z):
        up = pltpu.roll(z, LANES - ROPE_HALF, 1)
        down = pltpu.roll(z, ROPE_HALF, 1)
        return z * cos + up * sin_from_x2 + down * sin_from_x1

    k_rope = rope(k_rope)
    for h in range(N_HEADS):
        sl = slice(h * HEAD_PAD, (h + 1) * HEAD_PAD)
        q_ref[:, sl] = (rope(q_all[:, sl]) * Q_SCALE).astype(bf)
        k_ref[:, sl] = (k_all[:, sl] + k_rope).astype(bf)


def _qkv_call(x, pos, invf, g, win, gq, wuq, gkv, wuk, wuvt):
    s = x.shape[0]
    tm = V_CHUNK
    row = lambda i: (i, 0)
    fixed = lambda i: (0, 0)
    full = lambda arr: pl.BlockSpec(arr.shape, fixed)
    qk = jax.ShapeDtypeStruct((s, N_HEADS * HEAD_PAD), jnp.bfloat16)
    vt = jax.ShapeDtypeStruct((s // tm, N_HEADS * V_ROWS, tm), jnp.bfloat16)
    return pl.pallas_call(
        _qkv_kernel,
        grid=(s // tm,),
        in_specs=[pl.BlockSpec((tm, D_MODEL), row), pl.BlockSpec((tm, 1), row), full(invf),
                  full(g), full(win), full(gq), full(wuq), full(gkv), full(wuk), full(wuvt)],
        out_specs=[pl.BlockSpec((tm, N_HEADS * HEAD_PAD), row),
                   pl.BlockSpec((tm, N_HEADS * HEAD_PAD), row),
                   pl.BlockSpec((1, N_HEADS * V_ROWS, tm), lambda i: (i, 0, 0))],
        out_shape=[qk, qk, vt],
        compiler_params=pltpu.CompilerParams(
            dimension_semantics=("arbitrary",), vmem_limit_bytes=VMEM_LIMIT),
        name="qkv_proj",
    )(x, pos, invf, g, win, gq, wuq, gkv, wuk, wuvt)


class _Unit(NamedTuple):
    head: int
    col: int
    kv_start: Any
    width: int
    shift: Optional[int]


def _attn_kernel(q_ref, k_ref, vt_ref, o_ref, m_ref, acc_ref, *s_bufs, tk, td):
    tq = q_ref.shape[0]
    i = pl.program_id(1)
    m_ref[...] = jnp.full(m_ref.shape, MASK_VALUE, jnp.float32)
    acc_ref[...] = jnp.zeros(acc_ref.shape, jnp.float32)

    def scores(u, slot):
        sl = slice(u.head * HEAD_PAD, (u.head + 1) * HEAD_PAD)
        s = lax.dot_general(k_ref[pl.ds(u.kv_start, u.width), sl], q_ref[u.col:u.col + ATT_TILE, sl],
                            (((1,), (1,)), ((), ())), preferred_element_type=jnp.float32)
        if u.shift is not None:
            r = lax.broadcasted_iota(jnp.int32, s.shape, 0)
            c = lax.broadcasted_iota(jnp.int32, s.shape, 1)
            s = jnp.where(r <= c + u.shift, s, MASK_VALUE)
        s_bufs[slot][:u.width, :] = s
        rows = u.width // MAX_CHAINS
        parts = [jnp.max(s[g * rows:(g + 1) * rows], axis=0, keepdims=True)
                 for g in range(MAX_CHAINS)]
        while len(parts) > 1:
            parts = [jnp.maximum(a, b) for a, b in zip(parts[::2], parts[1::2])]
        return parts[0]

    def update(u, slot, col_max):
        qc = slice(u.col, u.col + ATT_TILE)
        m_prev = m_ref[u.head, :, qc]
        m_new = jnp.maximum(m_prev, col_max)
        alpha = jnp.exp2(m_prev - m_new)
        p = jnp.exp2(s_bufs[slot][:u.width, :] - m_new).astype(jnp.bfloat16)
        vrows = slice(u.head * V_ROWS, (u.head + 1) * V_ROWS)
        slab0 = u.kv_start // V_CHUNK
        pv = None
        for c, lo in enumerate(range(0, u.width, V_CHUNK)):
            n = min(V_CHUNK, u.width - lo)
            part = _dot(vt_ref[slab0 + c, vrows, :n], p[lo:lo + n])
            pv = part if pv is None else pv + part
        acc_ref[u.head, :, qc] = alpha * acc_ref[u.head, :, qc] + pv
        m_ref[u.head, :, qc] = m_new

    def run(units):
        n_buf = len(s_bufs)
        ahead = n_buf - 1
        maxes = {n: scores(units[n], n % n_buf) for n in range(min(ahead, len(units)))}
        for n, u in enumerate(units):
            if n + ahead < len(units):
                maxes[n + ahead] = scores(units[n + ahead], (n + ahead) % n_buf)
            update(u, n % n_buf, maxes.pop(n))

    def full_units(kv_start, width):
        return [_Unit(hh, c0, kv_start, width, None)
                for hh in range(2) for c0 in range(0, tq, ATT_TILE)]

    def body(j, carry):
        run(full_units(pl.multiple_of(j * tk, tk), tk))
        return carry

    q_start = i * tq
    n_main = q_start // tk
    lax.fori_loop(0, n_main, body, 0)
    if tk > tq:
        for d in range(tk // tq - 1):
            @pl.when(q_start - n_main * tk > d * tq)
            def _():
                run(full_units(pl.multiple_of(n_main * tk + d * tq, tq), tq))
    diag_start = pl.multiple_of(q_start, tq)
    run([_Unit(hh, c0, diag_start, (c0 // td + 1) * td, c0)
         for hh in range(2) for c0 in range(0, tq, ATT_TILE)])

    o_t = jnp.concatenate(
        [acc_ref[hh, :V_HEAD, :] / acc_ref[hh, V_HEAD:V_HEAD + 1, :] for hh in range(2)], axis=0)
    o_ref[...] = o_t.T.astype(o_ref.dtype)


def _attn_call(q, k, vt):
    s = q.shape[0]
    tq = ATT_Q
    pair = 2 * HEAD_PAD
    return pl.pallas_call(
        functools.partial(_attn_kernel, tk=ATT_K, td=ATT_DIAG),
        grid=(N_HEADS // 2, s // tq),
        in_specs=[pl.BlockSpec((tq, pair), lambda hp, i: (i, hp)),
                  pl.BlockSpec((s, pair), lambda hp, i: (0, hp)),
                  pl.BlockSpec((s // V_CHUNK, 2 * V_ROWS, V_CHUNK), lambda hp, i: (0, hp, 0))],
        out_specs=pl.BlockSpec((tq, 2 * V_HEAD), lambda hp, i: (i, hp)),
        out_shape=jax.ShapeDtypeStruct((s, N_HEADS * V_HEAD), jnp.bfloat16),
        scratch_shapes=[pltpu.VMEM((2, 1, tq), jnp.float32),
                        pltpu.VMEM((2, V_ROWS, tq), jnp.float32),
                        ] + [pltpu.VMEM((max(tq, ATT_K), ATT_TILE), jnp.float32)] * S_BUFS,
        compiler_params=pltpu.CompilerParams(
            dimension_semantics=("arbitrary", "arbitrary"), vmem_limit_bytes=VMEM_LIMIT),
        name="mla_attention",
    )(q, k, vt)


def _post_kernel(x_ref, halo_ref, attn_ref, p_ref, g_pre_ref, wpool_in_ref, wgate_ref, bgate_ref,
                 wpool_ref, pscale_ref, wba_ref, wbp_ref, wout_ref, g_post_ref, g_mlp_ref,
                 wff1_ref, wff2_ref, g_pmlp_ref, wpe_ref, wpg_ref, g_ple_ref, o_ref, ext_ref):
    bf = jnp.bfloat16
    tm = x_ref.shape[0]
    i = pl.program_id(0)
    x = x_ref[...]
    a = _rms(x, g_pre_ref[...]).astype(bf)

    a_halo = _rms(halo_ref[...], g_pre_ref[...]).astype(bf)
    u_halo = _dot(a_halo, wpool_in_ref[...])
    u = _dot(a, wpool_in_ref[...])
    ext_ref[0:POOL_HALO, :] = jnp.where(i > 0, u_halo, 0.0)
    ext_ref[POOL_HALO:, :] = u
    t = i * tm + lax.broadcasted_iota(jnp.int32, (tm, 1), 0)
    pooled = []
    for g, w in enumerate(POOL_WINDOWS):
        cols = slice(g * POOL_GROUP, (g + 1) * POOL_GROUP)
        wsum = ext_ref[POOL_HALO:POOL_HALO + tm, cols]
        for back in range(1, w):
            wsum = wsum + ext_ref[POOL_HALO - back:POOL_HALO - back + tm, cols]
        cnt = jnp.minimum(t + 1, w).astype(jnp.float32)
        d = wsum / cnt - u[:, cols]
        pooled.append(_dot(d.astype(bf), wpool_ref[g]))
    pooled = jnp.concatenate(pooled, axis=1) * pscale_ref[...]

    gates = jax.nn.sigmoid(_dot(a, wgate_ref[...]) + bgate_ref[...])
    merged = (gates[:, :D_MODEL] * _dot(attn_ref[...], wba_ref[...])
              + gates[:, D_MODEL:] * _dot(pooled.astype(bf), wbp_ref[...]))
    h = x + _rms(_dot(merged.astype(bf), wout_ref[...]), g_post_ref[...])

    m = _rms(h, g_mlp_ref[...]).astype(bf)
    f = jnp.zeros((tm, D_MODEL), jnp.float32)
    for c in range(D_FF // FF_CHUNK):
        cols = slice(c * FF_CHUNK, (c + 1) * FF_CHUNK)
        hid = jnp.square(jnp.maximum(_dot(m, wff1_ref[:, cols]), 0.0))
        f = f + _dot(hid.astype(bf), wff2_ref[cols, :])
    h = h + _rms(f, g_pmlp_ref[...])

    e = _dot(p_ref[...].astype(bf), wpe_ref[...])
    pg = jax.nn.sigmoid(_dot(h.astype(bf), wpg_ref[...]))
    o_ref[...] = h + _rms(pg * e, g_ple_ref[...])


def _post_call(x, attn, p, g_pre, wpool_in, wgate, bgate, wpool, pscale, wba, wbp, wout,
               g_post, g_mlp, wff1, wff2, g_pmlp, wpe, wpg, g_ple):
    s = x.shape[0]
    tm = POST_ROWS
    row = lambda i: (i, 0)
    halo_blocks = tm // POOL_HALO

    def const(arr):
        zeros = (0,) * arr.ndim
        return pl.BlockSpec(arr.shape, lambda i: zeros, pipeline_mode=pl.Buffered(1))

    weights = (g_pre, wpool_in, wgate, bgate, wpool, pscale, wba, wbp, wout, g_post, g_mlp,
               wff1, wff2, g_pmlp, wpe, wpg, g_ple)
    return pl.pallas_call(
        _post_kernel,
        grid=(s // tm,),
        in_specs=[pl.BlockSpec((tm, D_MODEL), row),
                  pl.BlockSpec((POOL_HALO, D_MODEL),
                               lambda i: (jnp.maximum(i * halo_blocks - 1, 0), 0)),
                  pl.BlockSpec((tm, N_HEADS * V_HEAD), row),
                  pl.BlockSpec((tm, PLE_DIM), row)] + [const(w) for w in weights],
        out_specs=pl.BlockSpec((tm, D_MODEL), row),
        out_shape=jax.ShapeDtypeStruct((s, D_MODEL), jnp.float32),
        scratch_shapes=[pltpu.VMEM((tm + POOL_HALO, POOL_WIDTH), jnp.float32)],
        compiler_params=pltpu.CompilerParams(
            dimension_semantics=("arbitrary",), vmem_limit_bytes=VMEM_LIMIT),
        name="post_attention",
    )(x, x, attn, p, *weights)


def _pad_heads(w, per_head, lane0=0):
    kdim = w.shape[0]
    w = w.reshape(kdim, N_HEADS, per_head)
    w = jnp.pad(w, ((0, 0), (0, 0), (lane0, HEAD_PAD - per_head - lane0)))
    return w.reshape(kdim, N_HEADS * HEAD_PAD)


def _layer(h, p, pos, invf, g_pre_mix, w_in, b_gate, g_q, w_uq, g_kv, w_ukv, w_pool, pool_scale,
           w_branch_attn, w_branch_pool, w_out, g_post_mix, g_pre_mlp, w_ff1, w_ff2, g_post_mlp,
           w_ple_proj, w_ple_gate, g_ple):
    bf = jnp.bfloat16
    row = lambda v: v.reshape(1, -1)
    o_kv = Q_LORA
    o_kr = o_kv + KV_LORA
    o_pool = o_kr + QK_ROPE
    o_gate = o_pool + POOL_WIDTH

    w_kr = jnp.pad(w_in[:, o_kr:o_pool], ((0, 0), (ROPE_LANE0, HEAD_PAD - ROPE_LANE0 - QK_ROPE)))
    w_in_qkv = jnp.concatenate([w_in[:, :o_kr], w_kr], axis=1).astype(bf)
    w_uq_p = _pad_heads(w_uq, QK_NOPE + QK_ROPE).astype(bf)
    w_ukv_h = w_ukv.reshape(KV_LORA, N_HEADS, QK_NOPE + V_HEAD)
    w_uk_p = _pad_heads(w_ukv_h[:, :, :QK_NOPE].reshape(KV_LORA, -1), QK_NOPE).astype(bf)
    w_uv_t = jnp.pad(jnp.transpose(w_ukv_h[:, :, QK_NOPE:], (1, 2, 0)),
                     ((0, 0), (0, V_ROWS - V_HEAD), (0, 0))).reshape(N_HEADS * V_ROWS, KV_LORA).astype(bf)

    q, k, vt = _qkv_call(h, pos, invf, row(g_pre_mix), w_in_qkv, row(g_q), w_uq_p, row(g_kv),
                         w_uk_p, w_uv_t)
    attn = _attn_call(q, k, vt)
    return _post_call(
        h, attn, p, row(g_pre_mix), w_in[:, o_pool:o_gate].astype(bf), w_in[:, o_gate:].astype(bf),
        row(b_gate), w_pool.astype(bf), row(pool_scale), w_branch_attn.astype(bf),
        w_branch_pool.astype(bf), w_out.astype(bf), row(g_post_mix), row(g_pre_mlp),
        w_ff1.astype(bf), w_ff2.astype(bf), row(g_post_mlp), w_ple_proj.astype(bf),
        w_ple_gate.astype(bf), row(g_ple))


@jax.jit
def kernel(x, p, positions, g_pre_mix, w_in, b_gate, g_q, w_uq, g_kv, w_ukv, w_pool, pool_scale,
           w_branch_attn, w_branch_pool, w_out, g_post_mix, g_pre_mlp, w_ff1, w_ff2, g_post_mlp,
           w_ple_proj, w_ple_gate, g_ple):
    batch, seq, _ = x.shape
    depth = w_in.shape[0]
    inv_freq = ROPE_THETA ** (-jnp.arange(0, QK_ROPE, 2, dtype=jnp.float32) / QK_ROPE)
    invf = jnp.zeros((1, HEAD_PAD), jnp.float32)
    invf = invf.at[0, ROPE_LANE0:ROPE_LANE0 + ROPE_HALF].set(inv_freq)
    invf = invf.at[0, ROPE_LANE0 + ROPE_HALF:ROPE_LANE0 + QK_ROPE].set(inv_freq)
    outs = []
    for b in range(batch):
        h = x[b]
        pos = positions[b].reshape(seq, 1)
        for l in range(depth):
            h = _layer(h, p[l, b], pos, invf, g_pre_mix[l], w_in[l], b_gate[l], g_q[l], w_uq[l],
                       g_kv[l], w_ukv[l], w_pool[l], pool_scale[l], w_branch_attn[l],
                       w_branch_pool[l], w_out[l], g_post_mix[l], g_pre_mlp[l], w_ff1[l], w_ff2[l],
                       g_post_mlp[l], w_ple_proj[l], w_ple_gate[l], g_ple[l])
        outs.append(h)
    return jnp.stack(outs, axis=0)
```

```python
import functools
import math
from typing import Any, NamedTuple, Optional

import jax
import jax.numpy as jnp
from jax import lax
from jax.experimental import pallas as pl
from jax.experimental.pallas import tpu as pltpu

D_MODEL = 1024
PLE_DIM = 256
N_HEADS = 8
QK_NOPE = 64
QK_ROPE = 32
V_HEAD = 64
Q_LORA = 384
KV_LORA = 256
POOL_WINDOWS = (2, 4, 8, 16)
POOL_GROUP = 128
POOL_WIDTH = POOL_GROUP * len(POOL_WINDOWS)
D_FF = 4 * D_MODEL
ROPE_THETA = 10000.0
EPS = 1e-6

LANES = 128
HEAD_PAD = LANES
ROPE_LANE0 = QK_NOPE
ROPE_HALF = QK_ROPE // 2
POOL_HALO = max(POOL_WINDOWS)
MASK_VALUE = -1e30

V_ROWS = 80
V_CHUNK = 512
ATT_Q = 2048
ATT_K = 1024
ATT_DIAG = 256
ATT_TILE = 256
MAX_CHAINS = 8
S_BUFS = 4
POST_ROWS = 512
FF_CHUNK = 1024
VMEM_LIMIT = 60 * 1024 * 1024

Q_SCALE = (QK_NOPE + QK_ROPE) ** -0.5 * math.log2(math.e)


def _rms(x, g):
    y = x * lax.rsqrt(jnp.mean(x * x, axis=-1, keepdims=True) + EPS)
    return y * g


def _dot(a, b):
    return jnp.dot(a, b, preferred_element_type=jnp.float32)


def _qkv_kernel(x_ref, pos_ref, invf_ref, g_ref, win_ref, gq_ref, wuq_ref, gkv_ref,
                wuk_ref, wuvt_ref, q_ref, k_ref, vt_ref):
    bf = jnp.bfloat16
    a = _rms(x_ref[...], g_ref[...]).astype(bf)
    proj = _dot(a, win_ref[...])
    q_down = proj[:, :Q_LORA]
    kv_down = proj[:, Q_LORA:Q_LORA + KV_LORA]
    k_rope = proj[:, Q_LORA + KV_LORA:]

    q_all = _dot(_rms(q_down, gq_ref[...]).astype(bf), wuq_ref[...])
    kv_n = _rms(kv_down, gkv_ref[...]).astype(bf)
    k_all = _dot(kv_n, wuk_ref[...])
    v_t = lax.dot_general(wuvt_ref[...], kv_n, (((1,), (1,)), ((), ())),
                          preferred_element_type=jnp.float32)
    v_row = lax.broadcasted_iota(jnp.int32, v_t.shape, 0) % V_ROWS
    vt_ref[0] = jnp.where(v_row == V_HEAD, 1.0, v_t).astype(bf)

    ang = pos_ref[...].astype(jnp.float32) * invf_ref[...]
    lane = lax.broadcasted_iota(jnp.int32, ang.shape, 1)
    in_x1 = (lane >= ROPE_LANE0) & (lane < ROPE_LANE0 + ROPE_HALF)
    in_x2 = (lane >= ROPE_LANE0 + ROPE_HALF) & (lane < ROPE_LANE0 + QK_ROPE)
    cos = jnp.where(in_x1 | in_x2, jnp.cos(ang), 1.0)
    sin = jnp.sin(ang)
    sin_from_x2 = jnp.where(in_x1, -sin, 0.0)
    sin_from_x1 = jnp.where(in_x2, sin, 0.0)

    def rope(z):
        up = pltpu.roll(z, LANES - ROPE_HALF, 1)
        down = pltpu.roll(z, ROPE_HALF, 1)
        return z * cos + up * sin_from_x2 + down * sin_from_x1

    k_rope = rope(k_rope)
    for h in range(N_HEADS):
        sl = slice(h * HEAD_PAD, (h + 1) * HEAD_PAD)
        q_ref[:, sl] = (rope(q_all[:, sl]) * Q_SCALE).astype(bf)
        k_ref[:, sl] = (k_all[:, sl] + k_rope).astype(bf)


def _qkv_call(x, pos, invf, g, win, gq, wuq, gkv, wuk, wuvt):
    s = x.shape[0]
    tm = V_CHUNK
    row = lambda i: (i, 0)
    fixed = lambda i: (0, 0)
    full = lambda arr: pl.BlockSpec(arr.shape, fixed)
    qk = jax.ShapeDtypeStruct((s, N_HEADS * HEAD_PAD), jnp.bfloat16)
    vt = jax.ShapeDtypeStruct((s // tm, N_HEADS * V_ROWS, tm), jnp.bfloat16)
    return pl.pallas_call(
        _qkv_kernel,
        grid=(s // tm,),
        in_specs=[pl.BlockSpec((tm, D_MODEL), row), pl.BlockSpec((tm, 1), row), full(invf),
                  full(g), full(win), full(gq), full(wuq), full(gkv), full(wuk), full(wuvt)],
        out_specs=[pl.BlockSpec((tm, N_HEADS * HEAD_PAD), row),
                   pl.BlockSpec((tm, N_HEADS * HEAD_PAD), row),
                   pl.BlockSpec((1, N_HEADS * V_ROWS, tm), lambda i: (i, 0, 0))],
        out_shape=[qk, qk, vt],
        compiler_params=pltpu.CompilerParams(
            dimension_semantics=("arbitrary",), vmem_limit_bytes=VMEM_LIMIT),
        name="qkv_proj",
    )(x, pos, invf, g, win, gq, wuq, gkv, wuk, wuvt)


class _Unit(NamedTuple):
    head: int
    col: int
    kv_start: Any
    width: int
    shift: Optional[int]


def _attn_kernel(q_ref, k_ref, vt_ref, o_ref, m_ref, acc_ref, *s_bufs, tk, td):
    tq = q_ref.shape[0]
    n_buf = len(s_bufs)
    ahead = n_buf - 1
    i = pl.program_id(1)
    m_ref[...] = jnp.full(m_ref.shape, MASK_VALUE, jnp.float32)
    acc_ref[...] = jnp.zeros(acc_ref.shape, jnp.float32)

    def scores(u, slot):
        sl = slice(u.head * HEAD_PAD, (u.head + 1) * HEAD_PAD)
        s = lax.dot_general(k_ref[pl.ds(u.kv_start, u.width), sl], q_ref[u.col:u.col + ATT_TILE, sl],
                            (((1,), (1,)), ((), ())), preferred_element_type=jnp.float32)
        if u.shift is not None:
            r = lax.broadcasted_iota(jnp.int32, s.shape, 0)
            c = lax.broadcasted_iota(jnp.int32, s.shape, 1)
            s = jnp.where(r <= c + u.shift, s, MASK_VALUE)
        s_bufs[slot][:u.width, :] = s
        rows = u.width // MAX_CHAINS
        parts = [jnp.max(s[g * rows:(g + 1) * rows], axis=0, keepdims=True)
                 for g in range(MAX_CHAINS)]
        while len(parts) > 1:
            parts = [jnp.maximum(a, b) for a, b in zip(parts[::2], parts[1::2])]
        return parts[0]

    def update(u, slot, col_max):
        qc = slice(u.col, u.col + ATT_TILE)
        m_prev = m_ref[u.head, :, qc]
        m_new = jnp.maximum(m_prev, col_max)
        alpha = jnp.exp2(m_prev - m_new)
        p = jnp.exp2(s_bufs[slot][:u.width, :] - m_new).astype(jnp.bfloat16)
        vrows = slice(u.head * V_ROWS, (u.head + 1) * V_ROWS)
        slab0 = u.kv_start // V_CHUNK
        pv = None
        for c, lo in enumerate(range(0, u.width, V_CHUNK)):
            n = min(V_CHUNK, u.width - lo)
            part = _dot(vt_ref[slab0 + c, vrows, :n], p[lo:lo + n])
            pv = part if pv is None else pv + part
        acc_ref[u.head, :, qc] = alpha * acc_ref[u.head, :, qc] + pv
        m_ref[u.head, :, qc] = m_new

    def run(units, ready=(), then=()):
        seq = list(units) + list(then)
        maxes = dict(enumerate(ready))
        for n in range(len(maxes), min(ahead, len(seq))):
            maxes[n] = scores(seq[n], n % n_buf)
        for n, u in enumerate(units):
            if n + ahead < len(seq):
                maxes[n + ahead] = scores(seq[n + ahead], (n + ahead) % n_buf)
            update(u, n % n_buf, maxes.pop(n))
        return tuple(maxes[len(units) + n] for n in range(len(then)))

    def full_units(block):
        kv_start = pl.multiple_of(block * tk, tk)
        return [_Unit(hh, c0, kv_start, tk, None)
                for hh in range(2) for c0 in range(0, tq, ATT_TILE)]

    n_main = (i * tq) // tk
    assert tq % tk == 0 and (2 * (tq // ATT_TILE)) % n_buf == 0

    def body(j, first_maxes):
        nxt = jnp.minimum(j + 1, n_main - 1)
        return run(full_units(j), ready=first_maxes, then=full_units(nxt)[:ahead])

    first = full_units(0)[:ahead]
    lax.fori_loop(0, n_main, body, tuple(scores(u, n) for n, u in enumerate(first)))

    diag_start = pl.multiple_of(i * tq, tq)
    run([_Unit(hh, c0, diag_start, (c0 // td + 1) * td, c0)
         for hh in range(2) for c0 in range(0, tq, ATT_TILE)])

    o_t = jnp.concatenate(
        [acc_ref[hh, :V_HEAD, :] / acc_ref[hh, V_HEAD:V_HEAD + 1, :] for hh in range(2)], axis=0)
    o_ref[...] = o_t.T.astype(o_ref.dtype)


def _attn_call(q, k, vt):
    s = q.shape[0]
    tq = ATT_Q
    pair = 2 * HEAD_PAD
    return pl.pallas_call(
        functools.partial(_attn_kernel, tk=ATT_K, td=ATT_DIAG),
        grid=(N_HEADS // 2, s // tq),
        in_specs=[pl.BlockSpec((tq, pair), lambda hp, i: (i, hp)),
                  pl.BlockSpec((s, pair), lambda hp, i: (0, hp)),
                  pl.BlockSpec((s // V_CHUNK, 2 * V_ROWS, V_CHUNK), lambda hp, i: (0, hp, 0))],
        out_specs=pl.BlockSpec((tq, 2 * V_HEAD), lambda hp, i: (i, hp)),
        out_shape=jax.ShapeDtypeStruct((s, N_HEADS * V_HEAD), jnp.bfloat16),
        scratch_shapes=[pltpu.VMEM((2, 1, tq), jnp.float32),
                        pltpu.VMEM((2, V_ROWS, tq), jnp.float32),
                        ] + [pltpu.VMEM((max(tq, ATT_K), ATT_TILE), jnp.float32)] * S_BUFS,
        compiler_params=pltpu.CompilerParams(
            dimension_semantics=("arbitrary", "arbitrary"), vmem_limit_bytes=VMEM_LIMIT),
        name="mla_attention",
    )(q, k, vt)


def _post_kernel(x_ref, halo_ref, attn_ref, p_ref, g_pre_ref, wpool_in_ref, wgate_ref, bgate_ref,
                 wpool_ref, pscale_ref, wba_ref, wbp_ref, wout_ref, g_post_ref, g_mlp_ref,
                 wff1_ref, wff2_ref, g_pmlp_ref, wpe_ref, wpg_ref, g_ple_ref, o_ref, ext_ref):
    bf = jnp.bfloat16
    tm = x_ref.shape[0]
    i = pl.program_id(0)
    x = x_ref[...]
    a = _rms(x, g_pre_ref[...]).astype(bf)

    a_halo = _rms(halo_ref[...], g_pre_ref[...]).astype(bf)
    u_halo = _dot(a_halo, wpool_in_ref[...])
    u = _dot(a, wpool_in_ref[...])
    ext_ref[0:POOL_HALO, :] = jnp.where(i > 0, u_halo, 0.0)
    ext_ref[POOL_HALO:, :] = u
    t = i * tm + lax.broadcasted_iota(jnp.int32, (tm, 1), 0)
    pooled = []
    for g, w in enumerate(POOL_WINDOWS):
        cols = slice(g * POOL_GROUP, (g + 1) * POOL_GROUP)
        wsum = ext_ref[POOL_HALO:POOL_HALO + tm, cols]
        for back in range(1, w):
            wsum = wsum + ext_ref[POOL_HALO - back:POOL_HALO - back + tm, cols]
        cnt = jnp.minimum(t + 1, w).astype(jnp.float32)
        d = wsum / cnt - u[:, cols]
        pooled.append(_dot(d.astype(bf), wpool_ref[g]))
    pooled = jnp.concatenate(pooled, axis=1) * pscale_ref[...]

    gates = jax.nn.sigmoid(_dot(a, wgate_ref[...]) + bgate_ref[...])
    merged = (gates[:, :D_MODEL] * _dot(attn_ref[...], wba_ref[...])
              + gates[:, D_MODEL:] * _dot(pooled.astype(bf), wbp_ref[...]))
    h = x + _rms(_dot(merged.astype(bf), wout_ref[...]), g_post_ref[...])

    m = _rms(h, g_mlp_ref[...]).astype(bf)
    f = jnp.zeros((tm, D_MODEL), jnp.float32)
    for c in range(D_FF // FF_CHUNK):
        cols = slice(c * FF_CHUNK, (c + 1) * FF_CHUNK)
        hid = jnp.square(jnp.maximum(_dot(m, wff1_ref[:, cols]), 0.0))
        f = f + _dot(hid.astype(bf), wff2_ref[cols, :])
    h = h + _rms(f, g_pmlp_ref[...])

    e = _dot(p_ref[...].astype(bf), wpe_ref[...])
    pg = jax.nn.sigmoid(_dot(h.astype(bf), wpg_ref[...]))
    o_ref[...] = h + _rms(pg * e, g_ple_ref[...])


def _post_call(x, attn, p, g_pre, wpool_in, wgate, bgate, wpool, pscale, wba, wbp, wout,
               g_post, g_mlp, wff1, wff2, g_pmlp, wpe, wpg, g_ple):
    s = x.shape[0]
    tm = POST_ROWS
    row = lambda i: (i, 0)
    halo_blocks = tm // POOL_HALO

    def const(arr):
        zeros = (0,) * arr.ndim
        return pl.BlockSpec(arr.shape, lambda i: zeros, pipeline_mode=pl.Buffered(1))

    weights = (g_pre, wpool_in, wgate, bgate, wpool, pscale, wba, wbp, wout, g_post, g_mlp,
               wff1, wff2, g_pmlp, wpe, wpg, g_ple)
    return pl.pallas_call(
        _post_kernel,
        grid=(s // tm,),
        in_specs=[pl.BlockSpec((tm, D_MODEL), row),
                  pl.BlockSpec((POOL_HALO, D_MODEL),
                               lambda i: (jnp.maximum(i * halo_blocks - 1, 0), 0)),
                  pl.BlockSpec((tm, N_HEADS * V_HEAD), row),
                  pl.BlockSpec((tm, PLE_DIM), row)] + [const(w) for w in weights],
        out_specs=pl.BlockSpec((tm, D_MODEL), row),
        out_shape=jax.ShapeDtypeStruct((s, D_MODEL), jnp.float32),
        scratch_shapes=[pltpu.VMEM((tm + POOL_HALO, POOL_WIDTH), jnp.float32)],
        compiler_params=pltpu.CompilerParams(
            dimension_semantics=("arbitrary",), vmem_limit_bytes=VMEM_LIMIT),
        name="post_attention",
    )(x, x, attn, p, *weights)


def _pad_heads(w, per_head, lane0=0):
    kdim = w.shape[0]
    w = w.reshape(kdim, N_HEADS, per_head)
    w = jnp.pad(w, ((0, 0), (0, 0), (lane0, HEAD_PAD - per_head - lane0)))
    return w.reshape(kdim, N_HEADS * HEAD_PAD)


def _layer(h, p, pos, invf, g_pre_mix, w_in, b_gate, g_q, w_uq, g_kv, w_ukv, w_pool, pool_scale,
           w_branch_attn, w_branch_pool, w_out, g_post_mix, g_pre_mlp, w_ff1, w_ff2, g_post_mlp,
           w_ple_proj, w_ple_gate, g_ple):
    bf = jnp.bfloat16
    row = lambda v: v.reshape(1, -1)
    o_kv = Q_LORA
    o_kr = o_kv + KV_LORA
    o_pool = o_kr + QK_ROPE
    o_gate = o_pool + POOL_WIDTH

    w_kr = jnp.pad(w_in[:, o_kr:o_pool], ((0, 0), (ROPE_LANE0, HEAD_PAD - ROPE_LANE0 - QK_ROPE)))
    w_in_qkv = jnp.concatenate([w_in[:, :o_kr], w_kr], axis=1).astype(bf)
    w_uq_p = _pad_heads(w_uq, QK_NOPE + QK_ROPE).astype(bf)
    w_ukv_h = w_ukv.reshape(KV_LORA, N_HEADS, QK_NOPE + V_HEAD)
    w_uk_p = _pad_heads(w_ukv_h[:, :, :QK_NOPE].reshape(KV_LORA, -1), QK_NOPE).astype(bf)
    w_uv_t = jnp.pad(jnp.transpose(w_ukv_h[:, :, QK_NOPE:], (1, 2, 0)),
                     ((0, 0), (0, V_ROWS - V_HEAD), (0, 0))).reshape(N_HEADS * V_ROWS, KV_LORA).astype(bf)

    q, k, vt = _qkv_call(h, pos, invf, row(g_pre_mix), w_in_qkv, row(g_q), w_uq_p, row(g_kv),
                         w_uk_p, w_uv_t)
    attn = _attn_call(q, k, vt)
    return _post_call(
        h, attn, p, row(g_pre_mix), w_in[:, o_pool:o_gate].astype(bf), w_in[:, o_gate:].astype(bf),
        row(b_gate), w_pool.astype(bf), row(pool_scale), w_branch_attn.astype(bf),
        w_branch_pool.astype(bf), w_out.astype(bf), row(g_post_mix), row(g_pre_mlp),
        w_ff1.astype(bf), w_ff2.astype(bf), row(g_post_mlp), w_ple_proj.astype(bf),
        w_ple_gate.astype(bf), row(g_ple))


@jax.jit
def kernel(x, p, positions, g_pre_mix, w_in, b_gate, g_q, w_uq, g_kv, w_ukv, w_pool, pool_scale,
           w_branch_attn, w_branch_pool, w_out, g_post_mix, g_pre_mlp, w_ff1, w_ff2, g_post_mlp,
           w_ple_proj, w_ple_gate, g_ple):
    batch, seq, _ = x.shape
    depth = w_in.shape[0]
    inv_freq = ROPE_THETA ** (-jnp.arange(0, QK_ROPE, 2, dtype=jnp.float32) / QK_ROPE)
    invf = jnp.zeros((1, HEAD_PAD), jnp.float32)
    invf = invf.at[0, ROPE_LANE0:ROPE_LANE0 + ROPE_HALF].set(inv_freq)
    invf = invf.at[0, ROPE_LANE0 + ROPE_HALF:ROPE_LANE0 + QK_ROPE].set(inv_freq)
    outs = []
    for b in range(batch):
        h = x[b]
        pos = positions[b].reshape(seq, 1)
        for l in range(depth):
            h = _layer(h, p[l, b], pos, invf, g_pre_mix[l], w_in[l], b_gate[l], g_q[l], w_uq[l],
                       g_kv[l], w_ukv[l], w_pool[l], pool_scale[l], w_branch_attn[l],
                       w_branch_pool[l], w_out[l], g_post_mix[l], g_pre_mlp[l], w_ff1[l], w_ff2[l],
                       g_post_mlp[l], w_ple_proj[l], w_ple_gate[l], g_ple[l])
        outs.append(h)
    return jnp.stack(outs, axis=0)
```

```python
import functools
import math
from typing import Any, NamedTuple, Optional

import jax
import jax.numpy as jnp
from jax import lax
from jax.experimental import pallas as pl
from jax.experimental.pallas import tpu as pltpu

D_MODEL = 1024
PLE_DIM = 256
N_HEADS = 8
QK_NOPE = 64
QK_ROPE = 32
V_HEAD = 64
Q_LORA = 384
KV_LORA = 256
POOL_WINDOWS = (2, 4, 8, 16)
POOL_GROUP = 128
POOL_WIDTH = POOL_GROUP * len(POOL_WINDOWS)
D_FF = 4 * D_MODEL
ROPE_THETA = 10000.0
EPS = 1e-6

LANES = 128
HEAD_PAD = LANES
ROPE_LANE0 = QK_NOPE
ROPE_HALF = QK_ROPE // 2
POOL_HALO = max(POOL_WINDOWS)
MASK_VALUE = -1e30

V_ROWS = 80
V_CHUNK = 512
ATT_Q = 2048
ATT_K = 1024
ATT_DIAG = 256
ATT_TILE = 256
MAX_CHAINS = 2
S_BUFS = 4
POST_ROWS = 512
POST_SPLIT = 2
FF_CHUNK = 1024
VMEM_LIMIT = 60 * 1024 * 1024

Q_SCALE = (QK_NOPE + QK_ROPE) ** -0.5 * math.log2(math.e)


def _rms(x, g):
    y = x * lax.rsqrt(jnp.mean(x * x, axis=-1, keepdims=True) + EPS)
    return y * g


def _dot(a, b):
    return jnp.dot(a, b, preferred_element_type=jnp.float32)


def _qkv_kernel(x_ref, pos_ref, invf_ref, g_ref, win_ref, gq_ref, wuq_ref, gkv_ref,
                wuk_ref, wuvt_ref, q_ref, k_ref, vt_ref):
    bf = jnp.bfloat16
    a = _rms(x_ref[...], g_ref[...]).astype(bf)
    proj = _dot(a, win_ref[...])
    q_down = proj[:, :Q_LORA]
    kv_down = proj[:, Q_LORA:Q_LORA + KV_LORA]
    k_rope = proj[:, Q_LORA + KV_LORA:]

    q_all = _dot(_rms(q_down, gq_ref[...]).astype(bf), wuq_ref[...])
    kv_n = _rms(kv_down, gkv_ref[...]).astype(bf)
    k_all = _dot(kv_n, wuk_ref[...])
    v_t = lax.dot_general(wuvt_ref[...], kv_n, (((1,), (1,)), ((), ())),
                          preferred_element_type=jnp.float32)
    v_row = lax.broadcasted_iota(jnp.int32, v_t.shape, 0) % V_ROWS
    vt_ref[0] = jnp.where(v_row == V_HEAD, 1.0, v_t).astype(bf)

    ang = pos_ref[...].astype(jnp.float32) * invf_ref[...]
    lane = lax.broadcasted_iota(jnp.int32, ang.shape, 1)
    in_x1 = (lane >= ROPE_LANE0) & (lane < ROPE_LANE0 + ROPE_HALF)
    in_x2 = (lane >= ROPE_LANE0 + ROPE_HALF) & (lane < ROPE_LANE0 + QK_ROPE)
    cos = jnp.where(in_x1 | in_x2, jnp.cos(ang), 1.0)
    sin = jnp.sin(ang)
    sin_from_x2 = jnp.where(in_x1, -sin, 0.0)
    sin_from_x1 = jnp.where(in_x2, sin, 0.0)

    def rope(z):
        up = pltpu.roll(z, LANES - ROPE_HALF, 1)
        down = pltpu.roll(z, ROPE_HALF, 1)
        return z * cos + up * sin_from_x2 + down * sin_from_x1

    k_rope = rope(k_rope)
    for h in range(N_HEADS):
        sl = slice(h * HEAD_PAD, (h + 1) * HEAD_PAD)
        q_ref[:, sl] = (rope(q_all[:, sl]) * Q_SCALE).astype(bf)
        k_ref[:, sl] = (k_all[:, sl] + k_rope).astype(bf)


def _qkv_call(x, pos, invf, g, win, gq, wuq, gkv, wuk, wuvt):
    s = x.shape[0]
    tm = V_CHUNK
    row = lambda i: (i, 0)
    fixed = lambda i: (0, 0)
    full = lambda arr: pl.BlockSpec(arr.shape, fixed)
    qk = jax.ShapeDtypeStruct((s, N_HEADS * HEAD_PAD), jnp.bfloat16)
    vt = jax.ShapeDtypeStruct((s // tm, N_HEADS * V_ROWS, tm), jnp.bfloat16)
    return pl.pallas_call(
        _qkv_kernel,
        grid=(s // tm,),
        in_specs=[pl.BlockSpec((tm, D_MODEL), row), pl.BlockSpec((tm, 1), row), full(invf),
                  full(g), full(win), full(gq), full(wuq), full(gkv), full(wuk), full(wuvt)],
        out_specs=[pl.BlockSpec((tm, N_HEADS * HEAD_PAD), row),
                   pl.BlockSpec((tm, N_HEADS * HEAD_PAD), row),
                   pl.BlockSpec((1, N_HEADS * V_ROWS, tm), lambda i: (i, 0, 0))],
        out_shape=[qk, qk, vt],
        compiler_params=pltpu.CompilerParams(
            dimension_semantics=("arbitrary",), vmem_limit_bytes=VMEM_LIMIT),
        name="qkv_proj",
    )(x, pos, invf, g, win, gq, wuq, gkv, wuk, wuvt)


class _Unit(NamedTuple):
    head: int
    col: int
    kv_start: Any
    width: int
    shift: Optional[int]


def _attn_kernel(q_ref, k_ref, vt_ref, o_ref, m_ref, acc_ref, *s_bufs, tk, td):
    tq = q_ref.shape[0]
    n_buf = len(s_bufs)
    ahead = n_buf - 1
    i = pl.program_id(1)
    m_ref[...] = jnp.full(m_ref.shape, MASK_VALUE, jnp.float32)
    acc_ref[...] = jnp.zeros(acc_ref.shape, jnp.float32)

    def scores(u, slot):
        sl = slice(u.head * HEAD_PAD, (u.head + 1) * HEAD_PAD)
        s = lax.dot_general(k_ref[pl.ds(u.kv_start, u.width), sl], q_ref[u.col:u.col + ATT_TILE, sl],
                            (((1,), (1,)), ((), ())), preferred_element_type=jnp.float32)
        if u.shift is not None:
            r = lax.broadcasted_iota(jnp.int32, s.shape, 0)
            c = lax.broadcasted_iota(jnp.int32, s.shape, 1)
            s = jnp.where(r <= c + u.shift, s, MASK_VALUE)
        s_bufs[slot][:u.width, :] = s
        rows = u.width // MAX_CHAINS
        parts = [jnp.max(s[g * rows:(g + 1) * rows], axis=0, keepdims=True)
                 for g in range(MAX_CHAINS)]
        while len(parts) > 1:
            parts = [jnp.maximum(a, b) for a, b in zip(parts[::2], parts[1::2])]
        return parts[0]

    def update(u, slot, col_max):
        qc = slice(u.col, u.col + ATT_TILE)
        m_prev = m_ref[u.head, :, qc]
        m_new = jnp.maximum(m_prev, col_max)
        alpha = jnp.exp2(m_prev - m_new)
        p = jnp.exp2(s_bufs[slot][:u.width, :] - m_new).astype(jnp.bfloat16)
        vrows = slice(u.head * V_ROWS, (u.head + 1) * V_ROWS)
        slab0 = u.kv_start // V_CHUNK
        pv = None
        for c, lo in enumerate(range(0, u.width, V_CHUNK)):
            n = min(V_CHUNK, u.width - lo)
            part = _dot(vt_ref[slab0 + c, vrows, :n], p[lo:lo + n])
            pv = part if pv is None else pv + part
        acc_ref[u.head, :, qc] = alpha * acc_ref[u.head, :, qc] + pv
        m_ref[u.head, :, qc] = m_new

    def run(units, ready=(), then=()):
        seq = list(units) + list(then)
        maxes = dict(enumerate(ready))
        for n in range(len(maxes), min(ahead, len(seq))):
            maxes[n] = scores(seq[n], n % n_buf)
        for n, u in enumerate(units):
            if n + ahead < len(seq):
                maxes[n + ahead] = scores(seq[n + ahead], (n + ahead) % n_buf)
            update(u, n % n_buf, maxes.pop(n))
        return tuple(maxes[len(units) + n] for n in range(len(then)))

    def full_units(block):
        kv_start = pl.multiple_of(block * tk, tk)
        return [_Unit(hh, c0, kv_start, tk, None)
                for hh in range(2) for c0 in range(0, tq, ATT_TILE)]

    n_main = (i * tq) // tk
    assert tq % tk == 0 and (2 * (tq // ATT_TILE)) % n_buf == 0

    def body(j, first_maxes):
        nxt = jnp.minimum(j + 1, n_main - 1)
        return run(full_units(j), ready=first_maxes, then=full_units(nxt)[:ahead])

    first = full_units(0)[:ahead]
    lax.fori_loop(0, n_main, body, tuple(scores(u, n) for n, u in enumerate(first)))

    diag_start = pl.multiple_of(i * tq, tq)
    run([_Unit(hh, c0, diag_start, (c0 // td + 1) * td, c0)
         for hh in range(2) for c0 in range(0, tq, ATT_TILE)])

    o_t = jnp.concatenate(
        [acc_ref[hh, :V_HEAD, :] / acc_ref[hh, V_HEAD:V_HEAD + 1, :] for hh in range(2)], axis=0)
    o_ref[...] = o_t.T.astype(o_ref.dtype)


def _attn_call(q, k, vt):
    s = q.shape[0]
    tq = ATT_Q
    pair = 2 * HEAD_PAD
    return pl.pallas_call(
        functools.partial(_attn_kernel, tk=ATT_K, td=ATT_DIAG),
        grid=(N_HEADS // 2, s // tq),
        in_specs=[pl.BlockSpec((tq, pair), lambda hp, i: (i, hp)),
                  pl.BlockSpec((s, pair), lambda hp, i: (0, hp)),
                  pl.BlockSpec((s // V_CHUNK, 2 * V_ROWS, V_CHUNK), lambda hp, i: (0, hp, 0))],
        out_specs=pl.BlockSpec((tq, 2 * V_HEAD), lambda hp, i: (i, hp)),
        out_shape=jax.ShapeDtypeStruct((s, N_HEADS * V_HEAD), jnp.bfloat16),
        scratch_shapes=[pltpu.VMEM((2, 1, tq), jnp.float32),
                        pltpu.VMEM((2, V_ROWS, tq), jnp.float32),
                        ] + [pltpu.VMEM((max(tq, ATT_K), ATT_TILE), jnp.float32)] * S_BUFS,
        compiler_params=pltpu.CompilerParams(
            dimension_semantics=("arbitrary", "arbitrary"), vmem_limit_bytes=VMEM_LIMIT),
        name="mla_attention",
    )(q, k, vt)


def _post_kernel(x_ref, halo_ref, attn_ref, p_ref, g_pre_ref, wpool_in_ref, wgate_ref, bgate_ref,
                 wpool_ref, pscale_ref, wba_ref, wbp_ref, wout_ref, g_post_ref, g_mlp_ref,
                 wff1_ref, wff2_ref, g_pmlp_ref, wpe_ref, wpg_ref, g_ple_ref, o_ref, ext_ref):
    bf = jnp.bfloat16
    tm = x_ref.shape[0]
    ts = tm // POST_SPLIT
    i = pl.program_id(0)

    def rows_chain(part):
        r0 = part * ts
        rows = slice(r0, r0 + ts)
        x = x_ref[rows, :]
        a = _rms(x, g_pre_ref[...]).astype(bf)
        yield

        if part == 0:
            x_halo, have_halo = halo_ref[...], i > 0
        else:
            x_halo, have_halo = x_ref[r0 - POOL_HALO:r0, :], True
        u_halo = _dot(_rms(x_halo, g_pre_ref[...]).astype(bf), wpool_in_ref[...])
        u = _dot(a, wpool_in_ref[...])
        ext = ext_ref.at[part]
        ext[0:POOL_HALO, :] = jnp.where(have_halo, u_halo, 0.0)
        ext[POOL_HALO:, :] = u
        yield
        gates = jax.nn.sigmoid(_dot(a, wgate_ref[...]) + bgate_ref[...])
        attn_branch = gates[:, :D_MODEL] * _dot(attn_ref[rows, :], wba_ref[...])
        yield
        t = i * tm + r0 + lax.broadcasted_iota(jnp.int32, (ts, 1), 0)
        pooled = []
        for g, w in enumerate(POOL_WINDOWS):
            cols = slice(g * POOL_GROUP, (g + 1) * POOL_GROUP)
            wsum = ext[:, cols]
            shift = 1
            while shift < w:
                wsum = wsum + pltpu.roll(wsum, shift, 0)
                shift *= 2
            wsum = wsum[POOL_HALO:, :]
            cnt = jnp.minimum(t + 1, w).astype(jnp.float32)
            d = wsum / cnt - u[:, cols]
            pooled.append(_dot(d.astype(bf), wpool_ref[g]))
        pooled = jnp.concatenate(pooled, axis=1) * pscale_ref[...]
        yield

        merged = attn_branch + gates[:, D_MODEL:] * _dot(pooled.astype(bf), wbp_ref[...])
        yield
        y = _dot(merged.astype(bf), wout_ref[...])
        yield
        h = x + _rms(y, g_post_ref[...])

        m = _rms(h, g_mlp_ref[...]).astype(bf)
        yield
        f = jnp.zeros((ts, D_MODEL), jnp.float32)
        for c in range(D_FF // FF_CHUNK):
            cols = slice(c * FF_CHUNK, (c + 1) * FF_CHUNK)
            hid = jnp.square(jnp.maximum(_dot(m, wff1_ref[:, cols]), 0.0))
            f = f + _dot(hid.astype(bf), wff2_ref[cols, :])
            yield
        h = h + _rms(f, g_pmlp_ref[...])
        yield

        e = _dot(p_ref[rows, :].astype(bf), wpe_ref[...])
        pg = jax.nn.sigmoid(_dot(h.astype(bf), wpg_ref[...]))
        yield
        o_ref[rows, :] = h + _rms(pg * e, g_ple_ref[...])

    chains = [rows_chain(part) for part in range(POST_SPLIT)]
    while chains:
        chains = [c for c in chains if next(c, True) is None]


def _post_call(x, attn, p, g_pre, wpool_in, wgate, bgate, wpool, pscale, wba, wbp, wout,
               g_post, g_mlp, wff1, wff2, g_pmlp, wpe, wpg, g_ple):
    s = x.shape[0]
    tm = POST_ROWS
    row = lambda i: (i, 0)
    halo_blocks = tm // POOL_HALO

    def const(arr):
        zeros = (0,) * arr.ndim
        return pl.BlockSpec(arr.shape, lambda i: zeros, pipeline_mode=pl.Buffered(1))

    weights = (g_pre, wpool_in, wgate, bgate, wpool, pscale, wba, wbp, wout, g_post, g_mlp,
               wff1, wff2, g_pmlp, wpe, wpg, g_ple)
    return pl.pallas_call(
        _post_kernel,
        grid=(s // tm,),
        in_specs=[pl.BlockSpec((tm, D_MODEL), row),
                  pl.BlockSpec((POOL_HALO, D_MODEL),
                               lambda i: (jnp.maximum(i * halo_blocks - 1, 0), 0)),
                  pl.BlockSpec((tm, N_HEADS * V_HEAD), row),
                  pl.BlockSpec((tm, PLE_DIM), row)] + [const(w) for w in weights],
        out_specs=pl.BlockSpec((tm, D_MODEL), row),
        out_shape=jax.ShapeDtypeStruct((s, D_MODEL), jnp.float32),
        scratch_shapes=[pltpu.VMEM((POST_SPLIT, tm // POST_SPLIT + POOL_HALO, POOL_WIDTH),
                                   jnp.float32)],
        compiler_params=pltpu.CompilerParams(
            dimension_semantics=("arbitrary",), vmem_limit_bytes=VMEM_LIMIT),
        name="post_attention",
    )(x, x, attn, p, *weights)


def _pad_heads(w, per_head, lane0=0):
    kdim = w.shape[0]
    w = w.reshape(kdim, N_HEADS, per_head)
    w = jnp.pad(w, ((0, 0), (0, 0), (lane0, HEAD_PAD - per_head - lane0)))
    return w.reshape(kdim, N_HEADS * HEAD_PAD)


def _layer(h, p, pos, invf, g_pre_mix, w_in, b_gate, g_q, w_uq, g_kv, w_ukv, w_pool, pool_scale,
           w_branch_attn, w_branch_pool, w_out, g_post_mix, g_pre_mlp, w_ff1, w_ff2, g_post_mlp,
           w_ple_proj, w_ple_gate, g_ple):
    bf = jnp.bfloat16
    row = lambda v: v.reshape(1, -1)
    o_kv = Q_LORA
    o_kr = o_kv + KV_LORA
    o_pool = o_kr + QK_ROPE
    o_gate = o_pool + POOL_WIDTH

    w_kr = jnp.pad(w_in[:, o_kr:o_pool], ((0, 0), (ROPE_LANE0, HEAD_PAD - ROPE_LANE0 - QK_ROPE)))
    w_in_qkv = jnp.concatenate([w_in[:, :o_kr], w_kr], axis=1).astype(bf)
    w_uq_p = _pad_heads(w_uq, QK_NOPE + QK_ROPE).astype(bf)
    w_ukv_h = w_ukv.reshape(KV_LORA, N_HEADS, QK_NOPE + V_HEAD)
    w_uk_p = _pad_heads(w_ukv_h[:, :, :QK_NOPE].reshape(KV_LORA, -1), QK_NOPE).astype(bf)
    w_uv_t = jnp.pad(jnp.transpose(w_ukv_h[:, :, QK_NOPE:], (1, 2, 0)),
                     ((0, 0), (0, V_ROWS - V_HEAD), (0, 0))).reshape(N_HEADS * V_ROWS, KV_LORA).astype(bf)

    q, k, vt = _qkv_call(h, pos, invf, row(g_pre_mix), w_in_qkv, row(g_q), w_uq_p, row(g_kv),
                         w_uk_p, w_uv_t)
    attn = _attn_call(q, k, vt)
    return _post_call(
        h, attn, p, row(g_pre_mix), w_in[:, o_pool:o_gate].astype(bf), w_in[:, o_gate:].astype(bf),
        row(b_gate), w_pool.astype(bf), row(pool_scale), w_branch_attn.astype(bf),
        w_branch_pool.astype(bf), w_out.astype(bf), row(g_post_mix), row(g_pre_mlp),
        w_ff1.astype(bf), w_ff2.astype(bf), row(g_post_mlp), w_ple_proj.astype(bf),
        w_ple_gate.astype(bf), row(g_ple))


@jax.jit
def kernel(x, p, positions, g_pre_mix, w_in, b_gate, g_q, w_uq, g_kv, w_ukv, w_pool, pool_scale,
           w_branch_attn, w_branch_pool, w_out, g_post_mix, g_pre_mlp, w_ff1, w_ff2, g_post_mlp,
           w_ple_proj, w_ple_gate, g_ple):
    batch, seq, _ = x.shape
    depth = w_in.shape[0]
    inv_freq = ROPE_THETA ** (-jnp.arange(0, QK_ROPE, 2, dtype=jnp.float32) / QK_ROPE)
    invf = jnp.zeros((1, HEAD_PAD), jnp.float32)
    invf = invf.at[0, ROPE_LANE0:ROPE_LANE0 + ROPE_HALF].set(inv_freq)
    invf = invf.at[0, ROPE_LANE0 + ROPE_HALF:ROPE_LANE0 + QK_ROPE].set(inv_freq)
    outs = []
    for b in range(batch):
        h = x[b]
        pos = positions[b].reshape(seq, 1)
        for l in range(depth):
            h = _layer(h, p[l, b], pos, invf, g_pre_mix[l], w_in[l], b_gate[l], g_q[l], w_uq[l],
                       g_kv[l], w_ukv[l], w_pool[l], pool_scale[l], w_branch_attn[l],
                       w_branch_pool[l], w_out[l], g_post_mix[l], g_pre_mlp[l], w_ff1[l], w_ff2[l],
                       g_post_mlp[l], w_ple_proj[l], w_ple_gate[l], g_ple[l])
        outs.append(h)
    return jnp.stack(outs, axis=0)
```

```python
import functools
import math
from typing import Any, NamedTuple, Optional

import jax
import jax.numpy as jnp
from jax import lax
from jax.experimental import pallas as pl
from jax.experimental.pallas import tpu as pltpu

D_MODEL = 1024
PLE_DIM = 256
N_HEADS = 8
QK_NOPE = 64
QK_ROPE = 32
V_HEAD = 64
Q_LORA = 384
KV_LORA = 256
POOL_WINDOWS = (2, 4, 8, 16)
POOL_GROUP = 128
POOL_WIDTH = POOL_GROUP * len(POOL_WINDOWS)
D_FF = 4 * D_MODEL
ROPE_THETA = 10000.0
EPS = 1e-6

LANES = 128
HEAD_PAD = LANES
ROPE_LANE0 = QK_NOPE
ROPE_HALF = QK_ROPE // 2
POOL_HALO = max(POOL_WINDOWS)
MASK_VALUE = -1e30

V_ROWS = 80
V_CHUNK = 512
QKV_SPLIT = 2
ROPE_PACK = LANES // QK_ROPE
ATT_Q = 2048
ATT_K = 1024
ATT_DIAG = 256
ATT_TILE = 256
MAX_CHAINS = 2
S_BUFS = 4
POST_ROWS = 512
POST_SPLIT = 2
FF_CHUNK = 1024
VMEM_LIMIT = 60 * 1024 * 1024

Q_SCALE = (QK_NOPE + QK_ROPE) ** -0.5 * math.log2(math.e)


def _rms(x, g):
    y = x * lax.rsqrt(jnp.mean(x * x, axis=-1, keepdims=True) + EPS)
    return y * g


def _dot(a, b):
    return jnp.dot(a, b, preferred_element_type=jnp.float32)


def _qkv_kernel(x_ref, pos_ref, invf_ref, g_ref, win_ref, gq_ref, wuq_ref, gkv_ref,
                wuk_ref, wuvt_ref, q_ref, k_ref, vt_ref):
    bf = jnp.bfloat16
    ts = x_ref.shape[0] // QKV_SPLIT
    tp = ts // ROPE_PACK

    def rows_chain(part):
        rows = slice(part * ts, (part + 1) * ts)
        a = _rms(x_ref[rows, :], g_ref[...]).astype(bf)
        yield
        proj = _dot(a, win_ref[...])
        yield
        q_n = _rms(proj[:, :Q_LORA], gq_ref[...]).astype(bf)
        kv_n = _rms(proj[:, Q_LORA:Q_LORA + KV_LORA], gkv_ref[...]).astype(bf)
        k_rope = proj[:, Q_LORA + KV_LORA:]
        yield
        q_all = _dot(q_n, wuq_ref[...])
        yield
        k_all = _dot(kv_n, wuk_ref[...])
        v_t = lax.dot_general(wuvt_ref[...], kv_n, (((1,), (1,)), ((), ())),
                              preferred_element_type=jnp.float32)
        v_row = lax.broadcasted_iota(jnp.int32, v_t.shape, 0) % V_ROWS
        vt_ref[0, :, rows] = jnp.where(v_row == V_HEAD, 1.0, v_t).astype(bf)
        yield

        ang = pos_ref[part * tp:(part + 1) * tp, :].astype(jnp.float32) * invf_ref[...]
        cos_p, sin_p = jnp.cos(ang), jnp.sin(ang)
        lane = lax.broadcasted_iota(jnp.int32, (tp, LANES), 1)
        in_x1 = (lane >= ROPE_LANE0) & (lane < ROPE_LANE0 + ROPE_HALF)
        in_x2 = (lane >= ROPE_LANE0 + ROPE_HALF) & (lane < ROPE_LANE0 + QK_ROPE)
        cos_t, sin_t = [], []
        for qr in range(ROPE_PACK):
            shift = (ROPE_LANE0 - QK_ROPE * qr) % LANES
            c = pltpu.roll(cos_p, shift, 1) if shift else cos_p
            s = pltpu.roll(sin_p, shift, 1) if shift else sin_p
            cos_t.append(jnp.where(lane < ROPE_LANE0, 1.0, jnp.where(in_x1 | in_x2, c, 0.0)))
            sin_t.append(jnp.where(in_x1, -s, jnp.where(in_x2, s, 0.0)))
        cos_t = jnp.concatenate(cos_t, axis=0)
        sin_t = jnp.concatenate(sin_t, axis=0)
        yield

        def rope(z):
            return z * cos_t + pltpu.roll(z, LANES - QK_ROPE, 1) * sin_t

        k_rope = rope(k_rope)
        for h in range(N_HEADS):
            sl = slice(h * HEAD_PAD, (h + 1) * HEAD_PAD)
            q_ref[rows, sl] = (rope(q_all[:, sl]) * Q_SCALE).astype(bf)
            k_ref[rows, sl] = (k_all[:, sl] + k_rope).astype(bf)
            if h % 2:
                yield

    chains = [rows_chain(part) for part in range(QKV_SPLIT)]
    while chains:
        chains = [c for c in chains if next(c, True) is None]


def _qkv_call(x, pos_packed, invf, g, win, gq, wuq, gkv, wuk, wuvt):
    s = x.shape[0]
    tm = V_CHUNK
    row = lambda i: (i, 0)
    fixed = lambda i: (0, 0)
    full = lambda arr: pl.BlockSpec(arr.shape, fixed)
    qk = jax.ShapeDtypeStruct((s, N_HEADS * HEAD_PAD), jnp.bfloat16)
    vt = jax.ShapeDtypeStruct((s // tm, N_HEADS * V_ROWS, tm), jnp.bfloat16)
    return pl.pallas_call(
        _qkv_kernel,
        grid=(s // tm,),
        in_specs=[pl.BlockSpec((tm, D_MODEL), row), pl.BlockSpec((tm // ROPE_PACK, LANES), row),
                  full(invf), full(g), full(win), full(gq), full(wuq), full(gkv), full(wuk),
                  full(wuvt)],
        out_specs=[pl.BlockSpec((tm, N_HEADS * HEAD_PAD), row),
                   pl.BlockSpec((tm, N_HEADS * HEAD_PAD), row),
                   pl.BlockSpec((1, N_HEADS * V_ROWS, tm), lambda i: (i, 0, 0))],
        out_shape=[qk, qk, vt],
        compiler_params=pltpu.CompilerParams(
            dimension_semantics=("arbitrary",), vmem_limit_bytes=VMEM_LIMIT),
        name="qkv_proj",
    )(x, pos_packed, invf, g, win, gq, wuq, gkv, wuk, wuvt)


class _Unit(NamedTuple):
    head: int
    col: int
    kv_start: Any
    width: int
    shift: Optional[int]


def _attn_kernel(q_ref, k_ref, vt_ref, o_ref, m_ref, acc_ref, *s_bufs, tk, td):
    tq = q_ref.shape[0]
    n_buf = len(s_bufs)
    ahead = n_buf - 1
    i = pl.program_id(1)
    m_ref[...] = jnp.full(m_ref.shape, MASK_VALUE, jnp.float32)
    acc_ref[...] = jnp.zeros(acc_ref.shape, jnp.float32)

    def scores(u, slot):
        sl = slice(u.head * HEAD_PAD, (u.head + 1) * HEAD_PAD)
        s = lax.dot_general(k_ref[pl.ds(u.kv_start, u.width), sl], q_ref[u.col:u.col + ATT_TILE, sl],
                            (((1,), (1,)), ((), ())), preferred_element_type=jnp.float32)
        if u.shift is not None:
            r = lax.broadcasted_iota(jnp.int32, s.shape, 0)
            c = lax.broadcasted_iota(jnp.int32, s.shape, 1)
            s = jnp.where(r <= c + u.shift, s, MASK_VALUE)
        s_bufs[slot][:u.width, :] = s
        rows = u.width // MAX_CHAINS
        parts = [jnp.max(s[g * rows:(g + 1) * rows], axis=0, keepdims=True)
                 for g in range(MAX_CHAINS)]
        while len(parts) > 1:
            parts = [jnp.maximum(a, b) for a, b in zip(parts[::2], parts[1::2])]
        return parts[0]

    def update(u, slot, col_max):
        qc = slice(u.col, u.col + ATT_TILE)
        m_prev = m_ref[u.head, :, qc]
        m_new = jnp.maximum(m_prev, col_max)
        alpha = jnp.exp2(m_prev - m_new)
        p = jnp.exp2(s_bufs[slot][:u.width, :] - m_new).astype(jnp.bfloat16)
        vrows = slice(u.head * V_ROWS, (u.head + 1) * V_ROWS)
        slab0 = u.kv_start // V_CHUNK
        pv = None
        for c, lo in enumerate(range(0, u.width, V_CHUNK)):
            n = min(V_CHUNK, u.width - lo)
            part = _dot(vt_ref[slab0 + c, vrows, :n], p[lo:lo + n])
            pv = part if pv is None else pv + part
        acc_ref[u.head, :, qc] = alpha * acc_ref[u.head, :, qc] + pv
        m_ref[u.head, :, qc] = m_new

    def run(units, ready=(), then=()):
        seq = list(units) + list(then)
        maxes = dict(enumerate(ready))
        for n in range(len(maxes), min(ahead, len(seq))):
            maxes[n] = scores(seq[n], n % n_buf)
        for n, u in enumerate(units):
            if n + ahead < len(seq):
                maxes[n + ahead] = scores(seq[n + ahead], (n + ahead) % n_buf)
            update(u, n % n_buf, maxes.pop(n))
        return tuple(maxes[len(units) + n] for n in range(len(then)))

    def full_units(block):
        kv_start = pl.multiple_of(block * tk, tk)
        return [_Unit(hh, c0, kv_start, tk, None)
                for hh in range(2) for c0 in range(0, tq, ATT_TILE)]

    n_main = (i * tq) // tk
    assert tq % tk == 0 and (2 * (tq // ATT_TILE)) % n_buf == 0

    def body(j, first_maxes):
        nxt = jnp.minimum(j + 1, n_main - 1)
        return run(full_units(j), ready=first_maxes, then=full_units(nxt)[:ahead])

    first = full_units(0)[:ahead]
    lax.fori_loop(0, n_main, body, tuple(scores(u, n) for n, u in enumerate(first)))

    diag_start = pl.multiple_of(i * tq, tq)
    run([_Unit(hh, c0, diag_start, (c0 // td + 1) * td, c0)
         for hh in range(2) for c0 in range(0, tq, ATT_TILE)])

    o_t = jnp.concatenate(
        [acc_ref[hh, :V_HEAD, :] / acc_ref[hh, V_HEAD:V_HEAD + 1, :] for hh in range(2)], axis=0)
    o_ref[...] = o_t.T.astype(o_ref.dtype)


def _attn_call(q, k, vt):
    s = q.shape[0]
    tq = ATT_Q
    pair = 2 * HEAD_PAD
    return pl.pallas_call(
        functools.partial(_attn_kernel, tk=ATT_K, td=ATT_DIAG),
        grid=(N_HEADS // 2, s // tq),
        in_specs=[pl.BlockSpec((tq, pair), lambda hp, i: (i, hp)),
                  pl.BlockSpec((s, pair), lambda hp, i: (0, hp)),
                  pl.BlockSpec((s // V_CHUNK, 2 * V_ROWS, V_CHUNK), lambda hp, i: (0, hp, 0))],
        out_specs=pl.BlockSpec((tq, 2 * V_HEAD), lambda hp, i: (i, hp)),
        out_shape=jax.ShapeDtypeStruct((s, N_HEADS * V_HEAD), jnp.bfloat16),
        scratch_shapes=[pltpu.VMEM((2, 1, tq), jnp.float32),
                        pltpu.VMEM((2, V_ROWS, tq), jnp.float32),
                        ] + [pltpu.VMEM((max(tq, ATT_K), ATT_TILE), jnp.float32)] * S_BUFS,
        compiler_params=pltpu.CompilerParams(
            dimension_semantics=("arbitrary", "arbitrary"), vmem_limit_bytes=VMEM_LIMIT),
        name="mla_attention",
    )(q, k, vt)


def _post_kernel(x_ref, halo_ref, attn_ref, p_ref, g_pre_ref, wpool_in_ref, wgate_ref, bgate_ref,
                 wpool_ref, pscale_ref, wba_ref, wbp_ref, wout_ref, g_post_ref, g_mlp_ref,
                 wff1_ref, wff2_ref, g_pmlp_ref, wpe_ref, wpg_ref, g_ple_ref, o_ref, ext_ref):
    bf = jnp.bfloat16
    tm = x_ref.shape[0]
    ts = tm // POST_SPLIT
    i = pl.program_id(0)

    def rows_chain(part):
        r0 = part * ts
        rows = slice(r0, r0 + ts)
        x = x_ref[rows, :]
        a = _rms(x, g_pre_ref[...]).astype(bf)
        yield

        if part == 0:
            x_halo, have_halo = halo_ref[...], i > 0
        else:
            x_halo, have_halo = x_ref[r0 - POOL_HALO:r0, :], True
        u_halo = _dot(_rms(x_halo, g_pre_ref[...]).astype(bf), wpool_in_ref[...])
        u = _dot(a, wpool_in_ref[...])
        ext = ext_ref.at[part]
        ext[0:POOL_HALO, :] = jnp.where(have_halo, u_halo, 0.0)
        ext[POOL_HALO:, :] = u
        yield
        gates = jax.nn.sigmoid(_dot(a, wgate_ref[...]) + bgate_ref[...])
        attn_branch = gates[:, :D_MODEL] * _dot(attn_ref[rows, :], wba_ref[...])
        yield
        t = i * tm + r0 + lax.broadcasted_iota(jnp.int32, (ts, 1), 0)
        pooled = []
        for g, w in enumerate(POOL_WINDOWS):
            cols = slice(g * POOL_GROUP, (g + 1) * POOL_GROUP)
            wsum = ext[:, cols]
            shift = 1
            while shift < w:
                wsum = wsum + pltpu.roll(wsum, shift, 0)
                shift *= 2
            wsum = wsum[POOL_HALO:, :]
            cnt = jnp.minimum(t + 1, w).astype(jnp.float32)
            d = wsum / cnt - u[:, cols]
            pooled.append(_dot(d.astype(bf), wpool_ref[g]))
        pooled = jnp.concatenate(pooled, axis=1) * pscale_ref[...]
        yield

        merged = attn_branch + gates[:, D_MODEL:] * _dot(pooled.astype(bf), wbp_ref[...])
        yield
        y = _dot(merged.astype(bf), wout_ref[...])
        yield
        h = x + _rms(y, g_post_ref[...])

        m = _rms(h, g_mlp_ref[...]).astype(bf)
        yield
        f = jnp.zeros((ts, D_MODEL), jnp.float32)
        for c in range(D_FF // FF_CHUNK):
            cols = slice(c * FF_CHUNK, (c + 1) * FF_CHUNK)
            hid = jnp.square(jnp.maximum(_dot(m, wff1_ref[:, cols]), 0.0))
            f = f + _dot(hid.astype(bf), wff2_ref[cols, :])
            yield
        h = h + _rms(f, g_pmlp_ref[...])
        yield

        e = _dot(p_ref[rows, :].astype(bf), wpe_ref[...])
        pg = jax.nn.sigmoid(_dot(h.astype(bf), wpg_ref[...]))
        yield
        o_ref[rows, :] = h + _rms(pg * e, g_ple_ref[...])

    chains = [rows_chain(part) for part in range(POST_SPLIT)]
    while chains:
        chains = [c for c in chains if next(c, True) is None]


def _post_call(x, attn, p, g_pre, wpool_in, wgate, bgate, wpool, pscale, wba, wbp, wout,
               g_post, g_mlp, wff1, wff2, g_pmlp, wpe, wpg, g_ple):
    s = x.shape[0]
    tm = POST_ROWS
    row = lambda i: (i, 0)
    halo_blocks = tm // POOL_HALO

    def const(arr):
        zeros = (0,) * arr.ndim
        return pl.BlockSpec(arr.shape, lambda i: zeros, pipeline_mode=pl.Buffered(1))

    weights = (g_pre, wpool_in, wgate, bgate, wpool, pscale, wba, wbp, wout, g_post, g_mlp,
               wff1, wff2, g_pmlp, wpe, wpg, g_ple)
    return pl.pallas_call(
        _post_kernel,
        grid=(s // tm,),
        in_specs=[pl.BlockSpec((tm, D_MODEL), row),
                  pl.BlockSpec((POOL_HALO, D_MODEL),
                               lambda i: (jnp.maximum(i * halo_blocks - 1, 0), 0)),
                  pl.BlockSpec((tm, N_HEADS * V_HEAD), row),
                  pl.BlockSpec((tm, PLE_DIM), row)] + [const(w) for w in weights],
        out_specs=pl.BlockSpec((tm, D_MODEL), row),
        out_shape=jax.ShapeDtypeStruct((s, D_MODEL), jnp.float32),
        scratch_shapes=[pltpu.VMEM((POST_SPLIT, tm // POST_SPLIT + POOL_HALO, POOL_WIDTH),
                                   jnp.float32)],
        compiler_params=pltpu.CompilerParams(
            dimension_semantics=("arbitrary",), vmem_limit_bytes=VMEM_LIMIT),
        name="post_attention",
    )(x, x, attn, p, *weights)


def _pad_heads(w, per_head, lane0=0):
    kdim = w.shape[0]
    w = w.reshape(kdim, N_HEADS, per_head)
    w = jnp.pad(w, ((0, 0), (0, 0), (lane0, HEAD_PAD - per_head - lane0)))
    return w.reshape(kdim, N_HEADS * HEAD_PAD)


def _rope_tile(nope, rope):
    x1, x2 = rope[..., :ROPE_HALF], rope[..., ROPE_HALF:]
    return jnp.concatenate([nope, x1, x2, x2, x1], axis=-1)


def _layer(h, p, pos, invf, g_pre_mix, w_in, b_gate, g_q, w_uq, g_kv, w_ukv, w_pool, pool_scale,
           w_branch_attn, w_branch_pool, w_out, g_post_mix, g_pre_mlp, w_ff1, w_ff2, g_post_mlp,
           w_ple_proj, w_ple_gate, g_ple):
    bf = jnp.bfloat16
    row = lambda v: v.reshape(1, -1)
    o_kv = Q_LORA
    o_kr = o_kv + KV_LORA
    o_pool = o_kr + QK_ROPE
    o_gate = o_pool + POOL_WIDTH

    w_kr = _rope_tile(jnp.zeros((D_MODEL, QK_NOPE), w_in.dtype), w_in[:, o_kr:o_pool])
    w_in_qkv = jnp.concatenate([w_in[:, :o_kr], w_kr], axis=1).astype(bf)
    w_uq_h = w_uq.reshape(Q_LORA, N_HEADS, QK_NOPE + QK_ROPE)
    w_uq_p = _rope_tile(w_uq_h[..., :QK_NOPE], w_uq_h[..., QK_NOPE:]).reshape(Q_LORA, -1).astype(bf)
    w_ukv_h = w_ukv.reshape(KV_LORA, N_HEADS, QK_NOPE + V_HEAD)
    w_uk_p = _pad_heads(w_ukv_h[:, :, :QK_NOPE].reshape(KV_LORA, -1), QK_NOPE).astype(bf)
    w_uv_t = jnp.pad(jnp.transpose(w_ukv_h[:, :, QK_NOPE:], (1, 2, 0)),
                     ((0, 0), (0, V_ROWS - V_HEAD), (0, 0))).reshape(N_HEADS * V_ROWS, KV_LORA).astype(bf)

    q, k, vt = _qkv_call(h, pos, invf, row(g_pre_mix), w_in_qkv, row(g_q), w_uq_p, row(g_kv),
                         w_uk_p, w_uv_t)
    attn = _attn_call(q, k, vt)
    return _post_call(
        h, attn, p, row(g_pre_mix), w_in[:, o_pool:o_gate].astype(bf), w_in[:, o_gate:].astype(bf),
        row(b_gate), w_pool.astype(bf), row(pool_scale), w_branch_attn.astype(bf),
        w_branch_pool.astype(bf), w_out.astype(bf), row(g_post_mix), row(g_pre_mlp),
        w_ff1.astype(bf), w_ff2.astype(bf), row(g_post_mlp), w_ple_proj.astype(bf),
        w_ple_gate.astype(bf), row(g_ple))


@jax.jit
def kernel(x, p, positions, g_pre_mix, w_in, b_gate, g_q, w_uq, g_kv, w_ukv, w_pool, pool_scale,
           w_branch_attn, w_branch_pool, w_out, g_post_mix, g_pre_mlp, w_ff1, w_ff2, g_post_mlp,
           w_ple_proj, w_ple_gate, g_ple):
    batch, seq, _ = x.shape
    depth = w_in.shape[0]
    inv_freq = ROPE_THETA ** (-jnp.arange(0, QK_ROPE, 2, dtype=jnp.float32) / QK_ROPE)
    invf = jnp.tile(inv_freq, 2 * ROPE_PACK).reshape(1, LANES)
    ts = V_CHUNK // QKV_SPLIT
    outs = []
    for b in range(batch):
        h = x[b]
        pos = positions[b].reshape(seq // ts, ROPE_PACK, ts // ROPE_PACK).transpose(0, 2, 1)
        pos = jnp.repeat(pos.reshape(seq // ROPE_PACK, ROPE_PACK), QK_ROPE, axis=1)
        for l in range(depth):
            h = _layer(h, p[l, b], pos, invf, g_pre_mix[l], w_in[l], b_gate[l], g_q[l], w_uq[l],
                       g_kv[l], w_ukv[l], w_pool[l], pool_scale[l], w_branch_attn[l],
                       w_branch_pool[l], w_out[l], g_post_mix[l], g_pre_mlp[l], w_ff1[l], w_ff2[l],
                       g_post_mlp[l], w_ple_proj[l], w_ple_gate[l], g_ple[l])
        outs.append(h)
    return jnp.stack(outs, axis=0)
```

```python
import functools
import math
from typing import Any, NamedTuple, Optional

import jax
import jax.numpy as jnp
from jax import lax
from jax.experimental import pallas as pl
from jax.experimental.pallas import tpu as pltpu

D_MODEL = 1024
PLE_DIM = 256
N_HEADS = 8
QK_NOPE = 64
QK_ROPE = 32
V_HEAD = 64
Q_LORA = 384
KV_LORA = 256
POOL_WINDOWS = (2, 4, 8, 16)
POOL_GROUP = 128
POOL_WIDTH = POOL_GROUP * len(POOL_WINDOWS)
D_FF = 4 * D_MODEL
ROPE_THETA = 10000.0
EPS = 1e-6

LANES = 128
HEAD_PAD = LANES
ROPE_LANE0 = QK_NOPE
ROPE_HALF = QK_ROPE // 2
POOL_HALO = max(POOL_WINDOWS)
MASK_VALUE = -1e30

V_ROWS = 80
V_CHUNK = 512
QKV_SPLIT = 2
ROPE_PACK = LANES // QK_ROPE
ATT_Q = 2048
ATT_K = 1024
ATT_DIAG = 256
ATT_TILE = 256
MAX_CHAINS = 2
S_BUFS = 4
POST_ROWS = 512
POST_SPLIT = 2
FF_CHUNK = 1024
VMEM_LIMIT = 60 * 1024 * 1024

Q_SCALE = (QK_NOPE + QK_ROPE) ** -0.5 * math.log2(math.e)


def _rms(x, g):
    y = x * lax.rsqrt(jnp.mean(x * x, axis=-1, keepdims=True) + EPS)
    return y * g


def _dot(a, b):
    return jnp.dot(a, b, preferred_element_type=jnp.float32)


def _qkv_kernel(x_ref, pos_ref, invf_ref, g_ref, win_ref, gq_ref, wuq_ref, gkv_ref,
                wuk_ref, wuvt_ref, q_ref, k_ref, vt_ref):
    bf = jnp.bfloat16
    ts = x_ref.shape[0] // QKV_SPLIT
    tp = ts // ROPE_PACK

    def rows_chain(part):
        rows = slice(part * ts, (part + 1) * ts)
        a = _rms(x_ref[rows, :], g_ref[...]).astype(bf)
        yield
        proj = _dot(a, win_ref[...])
        yield
        q_n = _rms(proj[:, :Q_LORA], gq_ref[...]).astype(bf)
        kv_n = _rms(proj[:, Q_LORA:Q_LORA + KV_LORA], gkv_ref[...]).astype(bf)
        k_rope = proj[:, Q_LORA + KV_LORA:]
        yield
        q_all = _dot(q_n, wuq_ref[...])
        yield
        k_all = _dot(kv_n, wuk_ref[...])
        v_t = lax.dot_general(wuvt_ref[...], kv_n, (((1,), (1,)), ((), ())),
                              preferred_element_type=jnp.float32)
        v_row = lax.broadcasted_iota(jnp.int32, v_t.shape, 0) % V_ROWS
        vt_ref[0, :, rows] = jnp.where(v_row == V_HEAD, 1.0, v_t).astype(bf)
        yield

        ang = pos_ref[part * tp:(part + 1) * tp, :].astype(jnp.float32) * invf_ref[...]
        cos_p, sin_p = jnp.cos(ang), jnp.sin(ang)
        lane = lax.broadcasted_iota(jnp.int32, (tp, LANES), 1)
        in_x1 = (lane >= ROPE_LANE0) & (lane < ROPE_LANE0 + ROPE_HALF)
        in_x2 = (lane >= ROPE_LANE0 + ROPE_HALF) & (lane < ROPE_LANE0 + QK_ROPE)
        cos_t, sin_t = [], []
        for qr in range(ROPE_PACK):
            shift = (ROPE_LANE0 - QK_ROPE * qr) % LANES
            c = pltpu.roll(cos_p, shift, 1) if shift else cos_p
            s = pltpu.roll(sin_p, shift, 1) if shift else sin_p
            cos_t.append(jnp.where(lane < ROPE_LANE0, 1.0, jnp.where(in_x1 | in_x2, c, 0.0)))
            sin_t.append(jnp.where(in_x1, -s, jnp.where(in_x2, s, 0.0)))
        cos_t = jnp.concatenate(cos_t, axis=0)
        sin_t = jnp.concatenate(sin_t, axis=0)
        yield

        def rope(z):
            return z * cos_t + pltpu.roll(z, LANES - QK_ROPE, 1) * sin_t

        k_rope = rope(k_rope)
        for h in range(N_HEADS):
            sl = slice(h * HEAD_PAD, (h + 1) * HEAD_PAD)
            q_ref[rows, sl] = (rope(q_all[:, sl]) * Q_SCALE).astype(bf)
            k_ref[rows, sl] = (k_all[:, sl] + k_rope).astype(bf)
            if h % 2:
                yield

    chains = [rows_chain(part) for part in range(QKV_SPLIT)]
    while chains:
        chains = [c for c in chains if next(c, True) is None]


def _qkv_call(x, pos_packed, invf, g, win, gq, wuq, gkv, wuk, wuvt):
    s = x.shape[0]
    tm = V_CHUNK
    row = lambda i: (i, 0)
    fixed = lambda i: (0, 0)
    full = lambda arr: pl.BlockSpec(arr.shape, fixed)
    qk = jax.ShapeDtypeStruct((s, N_HEADS * HEAD_PAD), jnp.bfloat16)
    vt = jax.ShapeDtypeStruct((s // tm, N_HEADS * V_ROWS, tm), jnp.bfloat16)
    return pl.pallas_call(
        _qkv_kernel,
        grid=(s // tm,),
        in_specs=[pl.BlockSpec((tm, D_MODEL), row), pl.BlockSpec((tm // ROPE_PACK, LANES), row),
                  full(invf), full(g), full(win), full(gq), full(wuq), full(gkv), full(wuk),
                  full(wuvt)],
        out_specs=[pl.BlockSpec((tm, N_HEADS * HEAD_PAD), row),
                   pl.BlockSpec((tm, N_HEADS * HEAD_PAD), row),
                   pl.BlockSpec((1, N_HEADS * V_ROWS, tm), lambda i: (i, 0, 0))],
        out_shape=[qk, qk, vt],
        compiler_params=pltpu.CompilerParams(
            dimension_semantics=("arbitrary",), vmem_limit_bytes=VMEM_LIMIT),
        name="qkv_proj",
    )(x, pos_packed, invf, g, win, gq, wuq, gkv, wuk, wuvt)


class _Unit(NamedTuple):
    head: int
    col: int
    kv_start: Any
    width: int
    shift: Optional[int]


def _attn_kernel(q_ref, k_ref, vt_ref, o_ref, m_ref, acc_ref, *s_bufs, tk, td):
    tq = q_ref.shape[0]
    n_buf = len(s_bufs)
    ahead = n_buf - 1
    i = pl.program_id(1)
    m_ref[...] = jnp.full(m_ref.shape, MASK_VALUE, jnp.float32)
    acc_ref[...] = jnp.zeros(acc_ref.shape, jnp.float32)

    def scores(u, slot):
        sl = slice(u.head * HEAD_PAD, (u.head + 1) * HEAD_PAD)
        s = lax.dot_general(k_ref[pl.ds(u.kv_start, u.width), sl], q_ref[u.col:u.col + ATT_TILE, sl],
                            (((1,), (1,)), ((), ())), preferred_element_type=jnp.float32)
        if u.shift is not None:
            r = lax.broadcasted_iota(jnp.int32, s.shape, 0)
            c = lax.broadcasted_iota(jnp.int32, s.shape, 1)
            s = jnp.where(r <= c + u.shift, s, MASK_VALUE)
        s_bufs[slot][:u.width, :] = s
        rows = u.width // MAX_CHAINS
        parts = [jnp.max(s[g * rows:(g + 1) * rows], axis=0, keepdims=True)
                 for g in range(MAX_CHAINS)]
        while len(parts) > 1:
            parts = [jnp.maximum(a, b) for a, b in zip(parts[::2], parts[1::2])]
        return parts[0]

    def update(u, slot, col_max):
        qc = slice(u.col, u.col + ATT_TILE)
        m_prev = m_ref[u.head, :, qc]
        m_new = jnp.maximum(m_prev, col_max)
        alpha = jnp.exp2(m_prev - m_new)
        p = jnp.exp2(s_bufs[slot][:u.width, :] - m_new).astype(jnp.bfloat16)
        vrows = slice(u.head * V_ROWS, (u.head + 1) * V_ROWS)
        slab0 = u.kv_start // V_CHUNK
        pv = None
        for c, lo in enumerate(range(0, u.width, V_CHUNK)):
            n = min(V_CHUNK, u.width - lo)
            part = _dot(vt_ref[slab0 + c, vrows, :n], p[lo:lo + n])
            pv = part if pv is None else pv + part
        acc_ref[u.head, :, qc] = alpha * acc_ref[u.head, :, qc] + pv
        m_ref[u.head, :, qc] = m_new

    def run(units, ready=(), then=(), after=None):
        seq = list(units) + list(then)
        maxes = dict(enumerate(ready))
        for n in range(len(maxes), min(ahead, len(seq))):
            maxes[n] = scores(seq[n], n % n_buf)
        for n, u in enumerate(units):
            if n + ahead < len(seq):
                maxes[n + ahead] = scores(seq[n + ahead], (n + ahead) % n_buf)
            update(u, n % n_buf, maxes.pop(n))
            if after is not None:
                after(u)
        return tuple(maxes[len(units) + n] for n in range(len(then)))

    tile_order = list(range(tk, tq, ATT_TILE)) + list(range(0, tk, ATT_TILE))

    def full_units(block):
        kv_start = pl.multiple_of(block * tk, tk)
        return [_Unit(hh, c0, kv_start, tk, None) for hh in range(2) for c0 in tile_order]

    n_main = (i * tq) // tk
    assert tq == 2 * tk and td == ATT_TILE and (2 * (tq // ATT_TILE)) % n_buf == 0 and ahead <= 4

    def body(j, first_maxes):
        return run(full_units(j), ready=first_maxes, then=full_units(j + 1)[:ahead])

    first = full_units(0)[:ahead]
    first_maxes = lax.fori_loop(0, n_main, body,
                                tuple(scores(u, n) for n, u in enumerate(first)))

    diag_start = pl.multiple_of(i * tq, tq)
    quarter = [u for u in full_units(n_main) if u.col >= tk]
    masked = []
    for c0 in range(tk - ATT_TILE, -ATT_TILE, -ATT_TILE):
        for hh in range(2):
            masked.append(_Unit(hh, c0, diag_start, c0 + ATT_TILE, c0))
        for hh in range(2):
            masked.append(_Unit(hh, c0 + tk, diag_start + tk, c0 + ATT_TILE, c0))

    def finish(u):
        if u.head != 1 or u.shift is None:
            return
        qc = slice(u.col, u.col + ATT_TILE)
        o_t = jnp.concatenate(
            [acc_ref[hh, :V_HEAD, qc] / acc_ref[hh, V_HEAD:V_HEAD + 1, qc] for hh in range(2)],
            axis=0)
        o_ref[qc, :] = o_t.T.astype(o_ref.dtype)

    run(quarter + masked, ready=first_maxes, after=finish)


def _attn_call(q, k, vt):
    s = q.shape[0]
    tq = ATT_Q
    pair = 2 * HEAD_PAD
    return pl.pallas_call(
        functools.partial(_attn_kernel, tk=ATT_K, td=ATT_DIAG),
        grid=(N_HEADS // 2, s // tq),
        in_specs=[pl.BlockSpec((tq, pair), lambda hp, i: (i, hp)),
                  pl.BlockSpec((s, pair), lambda hp, i: (0, hp)),
                  pl.BlockSpec((s // V_CHUNK, 2 * V_ROWS, V_CHUNK), lambda hp, i: (0, hp, 0))],
        out_specs=pl.BlockSpec((tq, 2 * V_HEAD), lambda hp, i: (i, hp)),
        out_shape=jax.ShapeDtypeStruct((s, N_HEADS * V_HEAD), jnp.bfloat16),
        scratch_shapes=[pltpu.VMEM((2, 1, tq), jnp.float32),
                        pltpu.VMEM((2, V_ROWS, tq), jnp.float32),
                        ] + [pltpu.VMEM((ATT_K, ATT_TILE), jnp.float32)] * S_BUFS,
        compiler_params=pltpu.CompilerParams(
            dimension_semantics=("arbitrary", "arbitrary"), vmem_limit_bytes=VMEM_LIMIT),
        name="mla_attention",
    )(q, k, vt)


def _post_kernel(x_ref, halo_ref, attn_ref, p_ref, g_pre_ref, wpool_in_ref, wgate_ref, bgate_ref,
                 wpool_ref, pscale_ref, wba_ref, wbp_ref, wout_ref, g_post_ref, g_mlp_ref,
                 wff1_ref, wff2_ref, g_pmlp_ref, wpe_ref, wpg_ref, g_ple_ref, o_ref, ext_ref):
    bf = jnp.bfloat16
    tm = x_ref.shape[0]
    ts = tm // POST_SPLIT
    i = pl.program_id(0)

    def rows_chain(part):
        r0 = part * ts
        rows = slice(r0, r0 + ts)
        x = x_ref[rows, :]
        a = _rms(x, g_pre_ref[...]).astype(bf)
        yield

        if part == 0:
            x_halo, have_halo = halo_ref[...], i > 0
        else:
            x_halo, have_halo = x_ref[r0 - POOL_HALO:r0, :], True
        u_halo = _dot(_rms(x_halo, g_pre_ref[...]).astype(bf), wpool_in_ref[...])
        u = _dot(a, wpool_in_ref[...])
        ext = ext_ref.at[part]
        ext[0:POOL_HALO, :] = jnp.where(have_halo, u_halo, 0.0)
        ext[POOL_HALO:, :] = u
        yield
        gates = jax.nn.sigmoid(_dot(a, wgate_ref[...]) + bgate_ref[...])
        attn_branch = gates[:, :D_MODEL] * _dot(attn_ref[rows, :], wba_ref[...])
        yield
        t = i * tm + r0 + lax.broadcasted_iota(jnp.int32, (ts, 1), 0)
        pooled = []
        for g, w in enumerate(POOL_WINDOWS):
            cols = slice(g * POOL_GROUP, (g + 1) * POOL_GROUP)
            wsum = ext[:, cols]
            shift = 1
            while shift < w:
                wsum = wsum + pltpu.roll(wsum, shift, 0)
                shift *= 2
            wsum = wsum[POOL_HALO:, :]
            cnt = jnp.minimum(t + 1, w).astype(jnp.float32)
            d = wsum / cnt - u[:, cols]
            pooled.append(_dot(d.astype(bf), wpool_ref[g]))
        pooled = jnp.concatenate(pooled, axis=1) * pscale_ref[...]
        yield

        merged = attn_branch + gates[:, D_MODEL:] * _dot(pooled.astype(bf), wbp_ref[...])
        yield
        y = _dot(merged.astype(bf), wout_ref[...])
        yield
        h = x + _rms(y, g_post_ref[...])

        m = _rms(h, g_mlp_ref[...]).astype(bf)
        yield
        f = jnp.zeros((ts, D_MODEL), jnp.float32)
        for c in range(D_FF // FF_CHUNK):
            cols = slice(c * FF_CHUNK, (c + 1) * FF_CHUNK)
            hid = jnp.square(jnp.maximum(_dot(m, wff1_ref[:, cols]), 0.0))
            f = f + _dot(hid.astype(bf), wff2_ref[cols, :])
            yield
        h = h + _rms(f, g_pmlp_ref[...])
        yield

        e = _dot(p_ref[rows, :].astype(bf), wpe_ref[...])
        pg = jax.nn.sigmoid(_dot(h.astype(bf), wpg_ref[...]))
        yield
        o_ref[rows, :] = h + _rms(pg * e, g_ple_ref[...])

    chains = [rows_chain(part) for part in range(POST_SPLIT)]
    while chains:
        chains = [c for c in chains if next(c, True) is None]


def _post_call(x, attn, p, g_pre, wpool_in, wgate, bgate, wpool, pscale, wba, wbp, wout,
               g_post, g_mlp, wff1, wff2, g_pmlp, wpe, wpg, g_ple):
    s = x.shape[0]
    tm = POST_ROWS
    row = lambda i: (i, 0)
    halo_blocks = tm // POOL_HALO

    def const(arr):
        zeros = (0,) * arr.ndim
        return pl.BlockSpec(arr.shape, lambda i: zeros, pipeline_mode=pl.Buffered(1))

    weights = (g_pre, wpool_in, wgate, bgate, wpool, pscale, wba, wbp, wout, g_post, g_mlp,
               wff1, wff2, g_pmlp, wpe, wpg, g_ple)
    return pl.pallas_call(
        _post_kernel,
        grid=(s // tm,),
        in_specs=[pl.BlockSpec((tm, D_MODEL), row),
                  pl.BlockSpec((POOL_HALO, D_MODEL),
                               lambda i: (jnp.maximum(i * halo_blocks - 1, 0), 0)),
                  pl.BlockSpec((tm, N_HEADS * V_HEAD), row),
                  pl.BlockSpec((tm, PLE_DIM), row)] + [const(w) for w in weights],
        out_specs=pl.BlockSpec((tm, D_MODEL), row),
        out_shape=jax.ShapeDtypeStruct((s, D_MODEL), jnp.float32),
        scratch_shapes=[pltpu.VMEM((POST_SPLIT, tm // POST_SPLIT + POOL_HALO, POOL_WIDTH),
                                   jnp.float32)],
        compiler_params=pltpu.CompilerParams(
            dimension_semantics=("arbitrary",), vmem_limit_bytes=VMEM_LIMIT),
        name="post_attention",
    )(x, x, attn, p, *weights)


def _pad_heads(w, per_head, lane0=0):
    kdim = w.shape[0]
    w = w.reshape(kdim, N_HEADS, per_head)
    w = jnp.pad(w, ((0, 0), (0, 0), (lane0, HEAD_PAD - per_head - lane0)))
    return w.reshape(kdim, N_HEADS * HEAD_PAD)


def _rope_tile(nope, rope):
    x1, x2 = rope[..., :ROPE_HALF], rope[..., ROPE_HALF:]
    return jnp.concatenate([nope, x1, x2, x2, x1], axis=-1)


def _layer(h, p, pos, invf, g_pre_mix, w_in, b_gate, g_q, w_uq, g_kv, w_ukv, w_pool, pool_scale,
           w_branch_attn, w_branch_pool, w_out, g_post_mix, g_pre_mlp, w_ff1, w_ff2, g_post_mlp,
           w_ple_proj, w_ple_gate, g_ple):
    bf = jnp.bfloat16
    row = lambda v: v.reshape(1, -1)
    o_kv = Q_LORA
    o_kr = o_kv + KV_LORA
    o_pool = o_kr + QK_ROPE
    o_gate = o_pool + POOL_WIDTH

    w_kr = _rope_tile(jnp.zeros((D_MODEL, QK_NOPE), w_in.dtype), w_in[:, o_kr:o_pool])
    w_in_qkv = jnp.concatenate([w_in[:, :o_kr], w_kr], axis=1).astype(bf)
    w_uq_h = w_uq.reshape(Q_LORA, N_HEADS, QK_NOPE + QK_ROPE)
    w_uq_p = _rope_tile(w_uq_h[..., :QK_NOPE], w_uq_h[..., QK_NOPE:]).reshape(Q_LORA, -1).astype(bf)
    w_ukv_h = w_ukv.reshape(KV_LORA, N_HEADS, QK_NOPE + V_HEAD)
    w_uk_p = _pad_heads(w_ukv_h[:, :, :QK_NOPE].reshape(KV_LORA, -1), QK_NOPE).astype(bf)
    w_uv_t = jnp.pad(jnp.transpose(w_ukv_h[:, :, QK_NOPE:], (1, 2, 0)),
                     ((0, 0), (0, V_ROWS - V_HEAD), (0, 0))).reshape(N_HEADS * V_ROWS, KV_LORA).astype(bf)

    q, k, vt = _qkv_call(h, pos, invf, row(g_pre_mix), w_in_qkv, row(g_q), w_uq_p, row(g_kv),
                         w_uk_p, w_uv_t)
    attn = _attn_call(q, k, vt)
    return _post_call(
        h, attn, p, row(g_pre_mix), w_in[:, o_pool:o_gate].astype(bf), w_in[:, o_gate:].astype(bf),
        row(b_gate), w_pool.astype(bf), row(pool_scale), w_branch_attn.astype(bf),
        w_branch_pool.astype(bf), w_out.astype(bf), row(g_post_mix), row(g_pre_mlp),
        w_ff1.astype(bf), w_ff2.astype(bf), row(g_post_mlp), w_ple_proj.astype(bf),
        w_ple_gate.astype(bf), row(g_ple))


@jax.jit
def kernel(x, p, positions, g_pre_mix, w_in, b_gate, g_q, w_uq, g_kv, w_ukv, w_pool, pool_scale,
           w_branch_attn, w_branch_pool, w_out, g_post_mix, g_pre_mlp, w_ff1, w_ff2, g_post_mlp,
           w_ple_proj, w_ple_gate, g_ple):
    batch, seq, _ = x.shape
    depth = w_in.shape[0]
    inv_freq = ROPE_THETA ** (-jnp.arange(0, QK_ROPE, 2, dtype=jnp.float32) / QK_ROPE)
    invf = jnp.tile(inv_freq, 2 * ROPE_PACK).reshape(1, LANES)
    ts = V_CHUNK // QKV_SPLIT
    outs = []
    for b in range(batch):
        h = x[b]
        pos = positions[b].reshape(seq // ts, ROPE_PACK, ts // ROPE_PACK).transpose(0, 2, 1)
        pos = jnp.repeat(pos.reshape(seq // ROPE_PACK, ROPE_PACK), QK_ROPE, axis=1)
        for l in range(depth):
            h = _layer(h, p[l, b], pos, invf, g_pre_mix[l], w_in[l], b_gate[l], g_q[l], w_uq[l],
                       g_kv[l], w_ukv[l], w_pool[l], pool_scale[l], w_branch_attn[l],
                       w_branch_pool[l], w_out[l], g_post_mix[l], g_pre_mlp[l], w_ff1[l], w_ff2[l],
                       g_post_mlp[l], w_ple_proj[l], w_ple_gate[l], g_ple[l])
        outs.append(h)
    return jnp.stack(outs, axis=0)
```

```python
import functools
import math
from typing import Any, NamedTuple, Optional

import jax
import jax.numpy as jnp
from jax import lax
from jax.experimental import pallas as pl
from jax.experimental.pallas import tpu as pltpu

D_MODEL = 1024
PLE_DIM = 256
N_HEADS = 8
QK_NOPE = 64
QK_ROPE = 32
V_HEAD = 64
Q_LORA = 384
KV_LORA = 256
POOL_WINDOWS = (2, 4, 8, 16)
POOL_GROUP = 128
POOL_WIDTH = POOL_GROUP * len(POOL_WINDOWS)
D_FF = 4 * D_MODEL
ROPE_THETA = 10000.0
EPS = 1e-6
IN_GATES = 2 * D_MODEL

LANES = 128
HEAD_PAD = LANES
ROPE_LANE0 = QK_NOPE
ROPE_HALF = QK_ROPE // 2
POOL_HALO = max(POOL_WINDOWS)
MASK_VALUE = -1e30

V_ROWS = 80
V_CHUNK = 512
W_IN_QKV_COLS = Q_LORA + KV_LORA + LANES
W_IN_QKV_BLOCK = -(-(IN_GATES + POOL_WIDTH) // W_IN_QKV_COLS)
QKV_SPLIT = 2
ROPE_PACK = LANES // QK_ROPE
ATT_Q = 4096
ATT_K = 2048
ATT_DIAG = 256
ATT_TILE = 256
MAX_CHAINS = 2
S_BUFS = 4
POST_ROWS = 512
POST_SPLIT = 2
FF_CHUNK = 1024
VMEM_LIMIT = 60 * 1024 * 1024

Q_SCALE = (QK_NOPE + QK_ROPE) ** -0.5 * math.log2(math.e)


def _rms(x, g):
    y = x * lax.rsqrt(jnp.mean(x * x, axis=-1, keepdims=True) + EPS)
    return y * g


def _dot(a, b):
    return jnp.dot(a, b, preferred_element_type=jnp.float32)


def _qkv_kernel(x_ref, pos_ref, invf_ref, g_ref, win_ref, gq_ref, wuq_ref, gkv_ref,
                wuk_ref, wuvt_ref, q_ref, k_ref, vt_ref):
    bf = jnp.bfloat16
    ts = x_ref.shape[0] // QKV_SPLIT
    tp = ts // ROPE_PACK

    def rows_chain(part):
        rows = slice(part * ts, (part + 1) * ts)
        a = _rms(x_ref[rows, :], g_ref[...]).astype(bf)
        yield
        proj = _dot(a, win_ref[...])
        yield
        q_n = _rms(proj[:, :Q_LORA], gq_ref[...]).astype(bf)
        kv_n = _rms(proj[:, Q_LORA:Q_LORA + KV_LORA], gkv_ref[...]).astype(bf)
        k_rope = proj[:, Q_LORA + KV_LORA:]
        yield
        q_all = _dot(q_n, wuq_ref[...])
        yield
        k_all = _dot(kv_n, wuk_ref[...])
        v_t = lax.dot_general(wuvt_ref[...], kv_n, (((1,), (1,)), ((), ())),
                              preferred_element_type=jnp.float32)
        v_row = lax.broadcasted_iota(jnp.int32, v_t.shape, 0) % V_ROWS
        vt_ref[0, :, rows] = jnp.where(v_row == V_HEAD, 1.0, v_t).astype(bf)
        yield

        ang = pos_ref[part * tp:(part + 1) * tp, :].astype(jnp.float32) * invf_ref[...]
        cos_p, sin_p = jnp.cos(ang), jnp.sin(ang)
        lane = lax.broadcasted_iota(jnp.int32, (tp, LANES), 1)
        in_x1 = (lane >= ROPE_LANE0) & (lane < ROPE_LANE0 + ROPE_HALF)
        in_x2 = (lane >= ROPE_LANE0 + ROPE_HALF) & (lane < ROPE_LANE0 + QK_ROPE)
        cos_t, sin_t = [], []
        for qr in range(ROPE_PACK):
            shift = (ROPE_LANE0 - QK_ROPE * qr) % LANES
            c = pltpu.roll(cos_p, shift, 1) if shift else cos_p
            s = pltpu.roll(sin_p, shift, 1) if shift else sin_p
            cos_t.append(jnp.where(lane < ROPE_LANE0, 1.0, jnp.where(in_x1 | in_x2, c, 0.0)))
            sin_t.append(jnp.where(in_x1, -s, jnp.where(in_x2, s, 0.0)))
        cos_t = jnp.concatenate(cos_t, axis=0)
        sin_t = jnp.concatenate(sin_t, axis=0)
        yield

        def rope(z):
            return z * cos_t + pltpu.roll(z, LANES - QK_ROPE, 1) * sin_t

        k_rope = rope(k_rope)
        for h in range(N_HEADS):
            sl = slice(h * HEAD_PAD, (h + 1) * HEAD_PAD)
            q_ref[rows, sl] = (rope(q_all[:, sl]) * Q_SCALE).astype(bf)
            k_ref[rows, sl] = (k_all[:, sl] + k_rope).astype(bf)
            if h % 2:
                yield

    chains = [rows_chain(part) for part in range(QKV_SPLIT)]
    while chains:
        chains = [c for c in chains if next(c, True) is None]


def _qkv_call(x, pos_packed, invf, g, win, gq, wuq, gkv, wuk, wuvt):
    s = x.shape[0]
    tm = V_CHUNK
    row = lambda i: (i, 0)
    fixed = lambda i: (0, 0)
    full = lambda arr: pl.BlockSpec(arr.shape, fixed)
    qk = jax.ShapeDtypeStruct((s, N_HEADS * HEAD_PAD), jnp.bfloat16)
    vt = jax.ShapeDtypeStruct((s // tm, N_HEADS * V_ROWS, tm), jnp.bfloat16)
    return pl.pallas_call(
        _qkv_kernel,
        grid=(s // tm,),
        in_specs=[pl.BlockSpec((tm, D_MODEL), row), pl.BlockSpec((tm // ROPE_PACK, LANES), row),
                  full(invf), full(g),
                  pl.BlockSpec((D_MODEL, W_IN_QKV_COLS), lambda i: (0, W_IN_QKV_BLOCK)),
                  full(gq), full(wuq), full(gkv), full(wuk), full(wuvt)],
        out_specs=[pl.BlockSpec((tm, N_HEADS * HEAD_PAD), row),
                   pl.BlockSpec((tm, N_HEADS * HEAD_PAD), row),
                   pl.BlockSpec((1, N_HEADS * V_ROWS, tm), lambda i: (i, 0, 0))],
        out_shape=[qk, qk, vt],
        compiler_params=pltpu.CompilerParams(
            dimension_semantics=("arbitrary",), vmem_limit_bytes=VMEM_LIMIT),
        name="qkv_proj",
    )(x, pos_packed, invf, g, win, gq, wuq, gkv, wuk, wuvt)


class _Unit(NamedTuple):
    head: int
    col: int
    kv_start: Any
    width: int
    shift: Optional[int]


def _attn_kernel(q_ref, k_ref, vt_ref, o_ref, m_ref, acc_ref, *s_bufs, tk, td):
    tq = q_ref.shape[0]
    n_buf = len(s_bufs)
    ahead = n_buf - 1
    i = pl.program_id(1)
    m_ref[...] = jnp.full(m_ref.shape, MASK_VALUE, jnp.float32)
    acc_ref[...] = jnp.zeros(acc_ref.shape, jnp.float32)

    def scores(u, slot):
        sl = slice(u.head * HEAD_PAD, (u.head + 1) * HEAD_PAD)
        s = lax.dot_general(k_ref[pl.ds(u.kv_start, u.width), sl], q_ref[u.col:u.col + ATT_TILE, sl],
                            (((1,), (1,)), ((), ())), preferred_element_type=jnp.float32)
        if u.shift is not None:
            r = lax.broadcasted_iota(jnp.int32, s.shape, 0)
            c = lax.broadcasted_iota(jnp.int32, s.shape, 1)
            s = jnp.where(r <= c + u.shift, s, MASK_VALUE)
        s_bufs[slot][:u.width, :] = s
        rows = u.width // MAX_CHAINS
        parts = [jnp.max(s[g * rows:(g + 1) * rows], axis=0, keepdims=True)
                 for g in range(MAX_CHAINS)]
        while len(parts) > 1:
            parts = [jnp.maximum(a, b) for a, b in zip(parts[::2], parts[1::2])]
        return parts[0]

    def update(u, slot, col_max):
        qc = slice(u.col, u.col + ATT_TILE)
        m_prev = m_ref[u.head, :, qc]
        m_new = jnp.maximum(m_prev, col_max)
        alpha = jnp.exp2(m_prev - m_new)
        p = jnp.exp2(s_bufs[slot][:u.width, :] - m_new).astype(jnp.bfloat16)
        vrows = slice(u.head * V_ROWS, (u.head + 1) * V_ROWS)
        slab0 = u.kv_start // V_CHUNK
        pv = None
        for c, lo in enumerate(range(0, u.width, V_CHUNK)):
            n = min(V_CHUNK, u.width - lo)
            part = _dot(vt_ref[slab0 + c, vrows, :n], p[lo:lo + n])
            pv = part if pv is None else pv + part
        acc_ref[u.head, :, qc] = alpha * acc_ref[u.head, :, qc] + pv
        m_ref[u.head, :, qc] = m_new

    def run(units, ready=(), then=(), after=None):
        seq = list(units) + list(then)
        maxes = dict(enumerate(ready))
        for n in range(len(maxes), min(ahead, len(seq))):
            maxes[n] = scores(seq[n], n % n_buf)
        for n, u in enumerate(units):
            if n + ahead < len(seq):
                maxes[n + ahead] = scores(seq[n + ahead], (n + ahead) % n_buf)
            update(u, n % n_buf, maxes.pop(n))
            if after is not None:
                after(u)
        return tuple(maxes[len(units) + n] for n in range(len(then)))

    tile_order = list(range(tk, tq, ATT_TILE)) + list(range(0, tk, ATT_TILE))

    def full_units(block):
        kv_start = pl.multiple_of(block * tk, tk)
        return [_Unit(hh, c0, kv_start, tk, None) for hh in range(2) for c0 in tile_order]

    n_main = (i * tq) // tk
    assert tq == 2 * tk and td == ATT_TILE and (2 * (tq // ATT_TILE)) % n_buf == 0 and ahead <= 4

    def body(j, first_maxes):
        return run(full_units(j), ready=first_maxes, then=full_units(j + 1)[:ahead])

    first = full_units(0)[:ahead]
    first_maxes = lax.fori_loop(0, n_main, body,
                                tuple(scores(u, n) for n, u in enumerate(first)))

    diag_start = pl.multiple_of(i * tq, tq)
    quarter = [u for u in full_units(n_main) if u.col >= tk]
    masked = []
    for c0 in range(tk - ATT_TILE, -ATT_TILE, -ATT_TILE):
        for hh in range(2):
            masked.append(_Unit(hh, c0, diag_start, c0 + ATT_TILE, c0))
        for hh in range(2):
            masked.append(_Unit(hh, c0 + tk, diag_start + tk, c0 + ATT_TILE, c0))

    def finish(u):
        if u.head != 1 or u.shift is None:
            return
        qc = slice(u.col, u.col + ATT_TILE)
        o_t = jnp.concatenate(
            [acc_ref[hh, :V_HEAD, qc] / acc_ref[hh, V_HEAD:V_HEAD + 1, qc] for hh in range(2)],
            axis=0)
        o_ref[qc, :] = o_t.T.astype(o_ref.dtype)

    run(quarter + masked, ready=first_maxes, after=finish)


def _attn_call(q, k, vt):
    s = q.shape[0]
    tq = ATT_Q
    pair = 2 * HEAD_PAD
    return pl.pallas_call(
        functools.partial(_attn_kernel, tk=ATT_K, td=ATT_DIAG),
        grid=(N_HEADS // 2, s // tq),
        in_specs=[pl.BlockSpec((tq, pair), lambda hp, i: (i, hp)),
                  pl.BlockSpec((s, pair), lambda hp, i: (0, hp)),
                  pl.BlockSpec((s // V_CHUNK, 2 * V_ROWS, V_CHUNK), lambda hp, i: (0, hp, 0))],
        out_specs=pl.BlockSpec((tq, 2 * V_HEAD), lambda hp, i: (i, hp)),
        out_shape=jax.ShapeDtypeStruct((s, N_HEADS * V_HEAD), jnp.bfloat16),
        scratch_shapes=[pltpu.VMEM((2, 1, tq), jnp.float32),
                        pltpu.VMEM((2, V_ROWS, tq), jnp.float32),
                        ] + [pltpu.VMEM((ATT_K, ATT_TILE), jnp.float32)] * S_BUFS,
        compiler_params=pltpu.CompilerParams(
            dimension_semantics=("arbitrary", "arbitrary"), vmem_limit_bytes=VMEM_LIMIT),
        name="mla_attention",
    )(q, k, vt)


def _post_kernel(x_ref, halo_ref, attn_ref, p_ref, g_pre_ref, wpool_in_ref, wgate_ref, bgate_ref,
                 wpool_ref, pscale_ref, wba_ref, wbp_ref, wout_ref, g_post_ref, g_mlp_ref,
                 wff1_ref, wff2_ref, g_pmlp_ref, wpe_ref, wpg_ref, g_ple_ref, o_ref, ext_ref):
    bf = jnp.bfloat16
    tm = x_ref.shape[0]
    ts = tm // POST_SPLIT
    i = pl.program_id(0)

    def rows_chain(part):
        r0 = part * ts
        rows = slice(r0, r0 + ts)
        x = x_ref[rows, :]
        a = _rms(x, g_pre_ref[...]).astype(bf)
        yield

        if part == 0:
            x_halo, have_halo = halo_ref[...], i > 0
        else:
            x_halo, have_halo = x_ref[r0 - POOL_HALO:r0, :], True
        u_halo = _dot(_rms(x_halo, g_pre_ref[...]).astype(bf), wpool_in_ref[...])
        u = _dot(a, wpool_in_ref[...])
        ext = ext_ref.at[part]
        ext[0:POOL_HALO, :] = jnp.where(have_halo, u_halo, 0.0)
        ext[POOL_HALO:, :] = u
        yield
        gates = jax.nn.sigmoid(_dot(a, wgate_ref[...]) + bgate_ref[...])
        attn_branch = gates[:, :D_MODEL] * _dot(attn_ref[rows, :], wba_ref[...])
        yield
        t = i * tm + r0 + lax.broadcasted_iota(jnp.int32, (ts, 1), 0)
        pooled = []
        for g, w in enumerate(POOL_WINDOWS):
            cols = slice(g * POOL_GROUP, (g + 1) * POOL_GROUP)
            wsum = ext[:, cols]
            shift = 1
            while shift < w:
                wsum = wsum + pltpu.roll(wsum, shift, 0)
                shift *= 2
            wsum = wsum[POOL_HALO:, :]
            cnt = jnp.minimum(t + 1, w).astype(jnp.float32)
            d = wsum / cnt - u[:, cols]
            pooled.append(_dot(d.astype(bf), wpool_ref[g]))
        pooled = jnp.concatenate(pooled, axis=1) * pscale_ref[...]
        yield

        merged = attn_branch + gates[:, D_MODEL:] * _dot(pooled.astype(bf), wbp_ref[...])
        yield
        y = _dot(merged.astype(bf), wout_ref[...])
        yield
        h = x + _rms(y, g_post_ref[...])

        m = _rms(h, g_mlp_ref[...]).astype(bf)
        yield
        f = jnp.zeros((ts, D_MODEL), jnp.float32)
        for c in range(D_FF // FF_CHUNK):
            cols = slice(c * FF_CHUNK, (c + 1) * FF_CHUNK)
            hid = jnp.square(jnp.maximum(_dot(m, wff1_ref[:, cols]), 0.0))
            f = f + _dot(hid.astype(bf), wff2_ref[cols, :])
            yield
        h = h + _rms(f, g_pmlp_ref[...])
        yield

        e = _dot(p_ref[rows, :].astype(bf), wpe_ref[...])
        pg = jax.nn.sigmoid(_dot(h.astype(bf), wpg_ref[...]))
        yield
        o_ref[rows, :] = h + _rms(pg * e, g_ple_ref[...])

    chains = [rows_chain(part) for part in range(POST_SPLIT)]
    while chains:
        chains = [c for c in chains if next(c, True) is None]


def _post_call(x, attn, p, g_pre, wpool_in, wgate, bgate, wpool, pscale, wba, wbp, wout,
               g_post, g_mlp, wff1, wff2, g_pmlp, wpe, wpg, g_ple):
    s = x.shape[0]
    tm = POST_ROWS
    row = lambda i: (i, 0)
    halo_blocks = tm // POOL_HALO

    def const(arr, shape=None, block=None):
        block = block or (0,) * arr.ndim
        return pl.BlockSpec(shape or arr.shape, lambda i: block, pipeline_mode=pl.Buffered(1))

    weights = (g_pre, wpool_in, wgate, bgate, wpool, pscale, wba, wbp, wout, g_post, g_mlp,
               wff1, wff2, g_pmlp, wpe, wpg, g_ple)
    specs = [const(w) for w in weights]
    specs[1] = const(wpool_in, (D_MODEL, POOL_WIDTH), (0, IN_GATES // POOL_WIDTH))
    specs[2] = const(wgate, (D_MODEL, IN_GATES), (0, 0))
    return pl.pallas_call(
        _post_kernel,
        grid=(s // tm,),
        in_specs=[pl.BlockSpec((tm, D_MODEL), row),
                  pl.BlockSpec((POOL_HALO, D_MODEL),
                               lambda i: (jnp.maximum(i * halo_blocks - 1, 0), 0)),
                  pl.BlockSpec((tm, N_HEADS * V_HEAD), row),
                  pl.BlockSpec((tm, PLE_DIM), row)] + specs,
        out_specs=pl.BlockSpec((tm, D_MODEL), row),
        out_shape=jax.ShapeDtypeStruct((s, D_MODEL), jnp.float32),
        scratch_shapes=[pltpu.VMEM((POST_SPLIT, tm // POST_SPLIT + POOL_HALO, POOL_WIDTH),
                                   jnp.float32)],
        compiler_params=pltpu.CompilerParams(
            dimension_semantics=("arbitrary",), vmem_limit_bytes=VMEM_LIMIT),
        name="post_attention",
    )(x, x, attn, p, *weights)


def _pad_heads(w, per_head, lane0=0):
    kdim = w.shape[0]
    w = w.reshape(kdim, N_HEADS, per_head)
    w = jnp.pad(w, ((0, 0), (0, 0), (lane0, HEAD_PAD - per_head - lane0)))
    return w.reshape(kdim, N_HEADS * HEAD_PAD)


def _rope_tile(nope, rope):
    x1, x2 = rope[..., :ROPE_HALF], rope[..., ROPE_HALF:]
    return jnp.concatenate([nope, x1, x2, x2, x1], axis=-1)


def _layer(h, p, pos, invf, g_pre_mix, w_in, b_gate, g_q, w_uq, g_kv, w_ukv, w_pool, pool_scale,
           w_branch_attn, w_branch_pool, w_out, g_post_mix, g_pre_mlp, w_ff1, w_ff2, g_post_mlp,
           w_ple_proj, w_ple_gate, g_ple):
    bf = jnp.bfloat16
    row = lambda v: v.reshape(1, -1)
    o_kv = Q_LORA
    o_kr = o_kv + KV_LORA
    o_pool = o_kr + QK_ROPE
    o_gate = o_pool + POOL_WIDTH

    w_kr = _rope_tile(jnp.zeros((D_MODEL, QK_NOPE), w_in.dtype), w_in[:, o_kr:o_pool])
    gap = W_IN_QKV_BLOCK * W_IN_QKV_COLS - (IN_GATES + POOL_WIDTH)
    w_in_all = jnp.concatenate(
        [w_in[:, o_gate:], w_in[:, o_pool:o_gate], jnp.zeros((D_MODEL, gap), w_in.dtype),
         w_in[:, :o_kr], w_kr], axis=1).astype(bf)
    w_uq_h = w_uq.reshape(Q_LORA, N_HEADS, QK_NOPE + QK_ROPE)
    w_uq_p = _rope_tile(w_uq_h[..., :QK_NOPE], w_uq_h[..., QK_NOPE:]).reshape(Q_LORA, -1).astype(bf)
    w_ukv_h = w_ukv.reshape(KV_LORA, N_HEADS, QK_NOPE + V_HEAD)
    w_uk_p = _pad_heads(w_ukv_h[:, :, :QK_NOPE].reshape(KV_LORA, -1), QK_NOPE).astype(bf)
    w_uv_t = jnp.pad(jnp.transpose(w_ukv_h[:, :, QK_NOPE:], (1, 2, 0)),
                     ((0, 0), (0, V_ROWS - V_HEAD), (0, 0))).reshape(N_HEADS * V_ROWS, KV_LORA).astype(bf)

    q, k, vt = _qkv_call(h, pos, invf, row(g_pre_mix), w_in_all, row(g_q), w_uq_p, row(g_kv),
                         w_uk_p, w_uv_t)
    attn = _attn_call(q, k, vt)
    return _post_call(
        h, attn, p, row(g_pre_mix), w_in_all, w_in_all,
        row(b_gate), w_pool.astype(bf), row(pool_scale), w_branch_attn.astype(bf),
        w_branch_pool.astype(bf), w_out.astype(bf), row(g_post_mix), row(g_pre_mlp),
        w_ff1.astype(bf), w_ff2.astype(bf), row(g_post_mlp), w_ple_proj.astype(bf),
        w_ple_gate.astype(bf), row(g_ple))


@jax.jit
def kernel(x, p, positions, g_pre_mix, w_in, b_gate, g_q, w_uq, g_kv, w_ukv, w_pool, pool_scale,
           w_branch_attn, w_branch_pool, w_out, g_post_mix, g_pre_mlp, w_ff1, w_ff2, g_post_mlp,
           w_ple_proj, w_ple_gate, g_ple):
    batch, seq, _ = x.shape
    depth = w_in.shape[0]
    inv_freq = ROPE_THETA ** (-jnp.arange(0, QK_ROPE, 2, dtype=jnp.float32) / QK_ROPE)
    invf = jnp.tile(inv_freq, 2 * ROPE_PACK).reshape(1, LANES)
    ts = V_CHUNK // QKV_SPLIT
    outs = []
    for b in range(batch):
        h = x[b]
        pos = positions[b].reshape(seq // ts, ROPE_PACK, ts // ROPE_PACK).transpose(0, 2, 1)
        pos = jnp.repeat(pos.reshape(seq // ROPE_PACK, ROPE_PACK), QK_ROPE, axis=1)
        for l in range(depth):
            h = _layer(h, p[l, b], pos, invf, g_pre_mix[l], w_in[l], b_gate[l], g_q[l], w_uq[l],
                       g_kv[l], w_ukv[l], w_pool[l], pool_scale[l], w_branch_attn[l],
                       w_branch_pool[l], w_out[l], g_post_mix[l], g_pre_mlp[l], w_ff1[l], w_ff2[l],
                       g_post_mlp[l], w_ple_proj[l], w_ple_gate[l], g_ple[l])
        outs.append(h)
    return jnp.stack(outs, axis=0)
```

```python
import functools
import math
from typing import Any, NamedTuple, Optional

import jax
import jax.numpy as jnp
from jax import lax
from jax.experimental import pallas as pl
from jax.experimental.pallas import tpu as pltpu

D_MODEL = 1024
PLE_DIM = 256
N_HEADS = 8
QK_NOPE = 64
QK_ROPE = 32
V_HEAD = 64
Q_LORA = 384
KV_LORA = 256
POOL_WINDOWS = (2, 4, 8, 16)
POOL_GROUP = 128
POOL_WIDTH = POOL_GROUP * len(POOL_WINDOWS)
D_FF = 4 * D_MODEL
ROPE_THETA = 10000.0
EPS = 1e-6

LANES = 128
HEAD_PAD = LANES
ROPE_LANE0 = QK_NOPE
ROPE_HALF = QK_ROPE // 2
POOL_HALO = max(POOL_WINDOWS)
MASK_VALUE = -1e30

V_ROWS = 80
V_CHUNK = 512
QKV_SPLIT = 2
ROPE_PACK = LANES // QK_ROPE
ATT_Q = 4096
ATT_K = 2048
ATT_DIAG = 256
ATT_TILE = 256
MAX_CHAINS = 2
S_BUFS = 4
POST_ROWS = 512
POST_SPLIT = 2
FF_CHUNK = 1024
VMEM_LIMIT = 60 * 1024 * 1024

Q_SCALE = (QK_NOPE + QK_ROPE) ** -0.5 * math.log2(math.e)


def _rms(x, g):
    y = x * lax.rsqrt(jnp.mean(x * x, axis=-1, keepdims=True) + EPS)
    return y * g


def _dot(a, b):
    return jnp.dot(a, b, preferred_element_type=jnp.float32)


def _qkv_kernel(x_ref, pos_ref, invf_ref, g_ref, win_ref, gq_ref, wuq_ref, gkv_ref,
                wuk_ref, wuvt_ref, q_ref, k_ref, vt_ref):
    bf = jnp.bfloat16
    ts = x_ref.shape[0] // QKV_SPLIT
    tp = ts // ROPE_PACK

    def rows_chain(part):
        rows = slice(part * ts, (part + 1) * ts)
        a = _rms(x_ref[rows, :], g_ref[...]).astype(bf)
        yield
        proj = _dot(a, win_ref[...])
        yield
        q_n = _rms(proj[:, :Q_LORA], gq_ref[...]).astype(bf)
        kv_n = _rms(proj[:, Q_LORA:Q_LORA + KV_LORA], gkv_ref[...]).astype(bf)
        k_rope = proj[:, Q_LORA + KV_LORA:]
        yield
        q_all = _dot(q_n, wuq_ref[...])
        yield
        k_all = _dot(kv_n, wuk_ref[...])
        v_t = lax.dot_general(wuvt_ref[...], kv_n, (((1,), (1,)), ((), ())),
                              preferred_element_type=jnp.float32)
        v_row = lax.broadcasted_iota(jnp.int32, v_t.shape, 0) % V_ROWS
        vt_ref[0, :, rows] = jnp.where(v_row == V_HEAD, 1.0, v_t).astype(bf)
        yield

        ang = pos_ref[part * tp:(part + 1) * tp, :].astype(jnp.float32) * invf_ref[...]
        cos_p, sin_p = jnp.cos(ang), jnp.sin(ang)
        lane = lax.broadcasted_iota(jnp.int32, (tp, LANES), 1)
        in_x1 = (lane >= ROPE_LANE0) & (lane < ROPE_LANE0 + ROPE_HALF)
        in_x2 = (lane >= ROPE_LANE0 + ROPE_HALF) & (lane < ROPE_LANE0 + QK_ROPE)
        cos_t, sin_t = [], []
        for qr in range(ROPE_PACK):
            shift = (ROPE_LANE0 - QK_ROPE * qr) % LANES
            c = pltpu.roll(cos_p, shift, 1) if shift else cos_p
            s = pltpu.roll(sin_p, shift, 1) if shift else sin_p
            cos_t.append(jnp.where(lane < ROPE_LANE0, 1.0, jnp.where(in_x1 | in_x2, c, 0.0)))
            sin_t.append(jnp.where(in_x1, -s, jnp.where(in_x2, s, 0.0)))
        cos_t = jnp.concatenate(cos_t, axis=0)
        sin_t = jnp.concatenate(sin_t, axis=0)
        yield

        def rope(z):
            return z * cos_t + pltpu.roll(z, LANES - QK_ROPE, 1) * sin_t

        k_rope = rope(k_rope)
        for h in range(N_HEADS):
            sl = slice(h * HEAD_PAD, (h + 1) * HEAD_PAD)
            q_ref[rows, sl] = (rope(q_all[:, sl]) * Q_SCALE).astype(bf)
            k_ref[rows, sl] = (k_all[:, sl] + k_rope).astype(bf)
            if h % 2:
                yield

    chains = [rows_chain(part) for part in range(QKV_SPLIT)]
    while chains:
        chains = [c for c in chains if next(c, True) is None]


def _qkv_call(x, pos_packed, invf, g, win, gq, wuq, gkv, wuk, wuvt):
    s = x.shape[0]
    tm = V_CHUNK
    row = lambda i: (i, 0)
    fixed = lambda i: (0, 0)
    full = lambda arr: pl.BlockSpec(arr.shape, fixed)
    qk = jax.ShapeDtypeStruct((s, N_HEADS * HEAD_PAD), jnp.bfloat16)
    vt = jax.ShapeDtypeStruct((s // tm, N_HEADS * V_ROWS, tm), jnp.bfloat16)
    return pl.pallas_call(
        _qkv_kernel,
        grid=(s // tm,),
        in_specs=[pl.BlockSpec((tm, D_MODEL), row), pl.BlockSpec((tm // ROPE_PACK, LANES), row),
                  full(invf), full(g), full(win), full(gq), full(wuq), full(gkv), full(wuk),
                  full(wuvt)],
        out_specs=[pl.BlockSpec((tm, N_HEADS * HEAD_PAD), row),
                   pl.BlockSpec((tm, N_HEADS * HEAD_PAD), row),
                   pl.BlockSpec((1, N_HEADS * V_ROWS, tm), lambda i: (i, 0, 0))],
        out_shape=[qk, qk, vt],
        compiler_params=pltpu.CompilerParams(
            dimension_semantics=("arbitrary",), vmem_limit_bytes=VMEM_LIMIT),
        name="qkv_proj",
    )(x, pos_packed, invf, g, win, gq, wuq, gkv, wuk, wuvt)


class _Unit(NamedTuple):
    head: int
    col: int
    kv_start: Any
    width: int
    shift: Optional[int]


def _attn_kernel(q_ref, k_ref, vt_ref, o_ref, m_ref, acc_ref, *s_bufs, tk, td):
    tq = q_ref.shape[0]
    n_buf = len(s_bufs)
    ahead = n_buf - 1
    i = pl.program_id(1)
    m_ref[...] = jnp.full(m_ref.shape, MASK_VALUE, jnp.float32)
    acc_ref[...] = jnp.zeros(acc_ref.shape, jnp.float32)

    def scores(u, slot):
        sl = slice(u.head * HEAD_PAD, (u.head + 1) * HEAD_PAD)
        s = lax.dot_general(k_ref[pl.ds(u.kv_start, u.width), sl], q_ref[u.col:u.col + ATT_TILE, sl],
                            (((1,), (1,)), ((), ())), preferred_element_type=jnp.float32)
        if u.shift is not None:
            r = lax.broadcasted_iota(jnp.int32, s.shape, 0)
            c = lax.broadcasted_iota(jnp.int32, s.shape, 1)
            s = jnp.where(r <= c + u.shift, s, MASK_VALUE)
        s_bufs[slot][:u.width, :] = s
        rows = u.width // MAX_CHAINS
        parts = [jnp.max(s[g * rows:(g + 1) * rows], axis=0, keepdims=True)
                 for g in range(MAX_CHAINS)]
        while len(parts) > 1:
            parts = [jnp.maximum(a, b) for a, b in zip(parts[::2], parts[1::2])]
        return parts[0]

    def update(u, slot, col_max):
        qc = slice(u.col, u.col + ATT_TILE)
        m_prev = m_ref[u.head, :, qc]
        m_new = jnp.maximum(m_prev, col_max)
        alpha = jnp.exp2(m_prev - m_new)
        p = jnp.exp2(s_bufs[slot][:u.width, :] - m_new).astype(jnp.bfloat16)
        vrows = slice(u.head * V_ROWS, (u.head + 1) * V_ROWS)
        slab0 = u.kv_start // V_CHUNK
        pv = None
        for c, lo in enumerate(range(0, u.width, V_CHUNK)):
            n = min(V_CHUNK, u.width - lo)
            part = _dot(vt_ref[slab0 + c, vrows, :n], p[lo:lo + n])
            pv = part if pv is None else pv + part
        acc_ref[u.head, :, qc] = alpha * acc_ref[u.head, :, qc] + pv
        m_ref[u.head, :, qc] = m_new

    def run(units, ready=(), then=(), after=None):
        seq = list(units) + list(then)
        maxes = dict(enumerate(ready))
        for n in range(len(maxes), min(ahead, len(seq))):
            maxes[n] = scores(seq[n], n % n_buf)
        for n, u in enumerate(units):
            if n + ahead < len(seq):
                maxes[n + ahead] = scores(seq[n + ahead], (n + ahead) % n_buf)
            update(u, n % n_buf, maxes.pop(n))
            if after is not None:
                after(u)
        return tuple(maxes[len(units) + n] for n in range(len(then)))

    tile_order = list(range(tk, tq, ATT_TILE)) + list(range(0, tk, ATT_TILE))

    def full_units(block):
        kv_start = pl.multiple_of(block * tk, tk)
        return [_Unit(hh, c0, kv_start, tk, None) for hh in range(2) for c0 in tile_order]

    n_main = (i * tq) // tk
    assert tq == 2 * tk and td == ATT_TILE and (2 * (tq // ATT_TILE)) % n_buf == 0 and ahead <= 4

    def body(j, first_maxes):
        return run(full_units(j), ready=first_maxes, then=full_units(j + 1)[:ahead])

    first = full_units(0)[:ahead]
    first_maxes = lax.fori_loop(0, n_main, body,
                                tuple(scores(u, n) for n, u in enumerate(first)))

    diag_start = pl.multiple_of(i * tq, tq)
    quarter = [u for u in full_units(n_main) if u.col >= tk]
    masked = []
    for c0 in range(tk - ATT_TILE, -ATT_TILE, -ATT_TILE):
        for hh in range(2):
            masked.append(_Unit(hh, c0, diag_start, c0 + ATT_TILE, c0))
        for hh in range(2):
            masked.append(_Unit(hh, c0 + tk, diag_start + tk, c0 + ATT_TILE, c0))

    def finish(u):
        if u.head != 1 or u.shift is None:
            return
        qc = slice(u.col, u.col + ATT_TILE)
        o_t = jnp.concatenate(
            [acc_ref[hh, :V_HEAD, qc] / acc_ref[hh, V_HEAD:V_HEAD + 1, qc] for hh in range(2)],
            axis=0)
        o_ref[qc, :] = o_t.T.astype(o_ref.dtype)

    run(quarter + masked, ready=first_maxes, after=finish)


def _attn_call(q, k, vt):
    s = q.shape[0]
    tq = ATT_Q
    pair = 2 * HEAD_PAD
    return pl.pallas_call(
        functools.partial(_attn_kernel, tk=ATT_K, td=ATT_DIAG),
        grid=(N_HEADS // 2, s // tq),
        in_specs=[pl.BlockSpec((tq, pair), lambda hp, i: (i, hp)),
                  pl.BlockSpec((s, pair), lambda hp, i: (0, hp)),
                  pl.BlockSpec((s // V_CHUNK, 2 * V_ROWS, V_CHUNK), lambda hp, i: (0, hp, 0))],
        out_specs=pl.BlockSpec((tq, 2 * V_HEAD), lambda hp, i: (i, hp)),
        out_shape=jax.ShapeDtypeStruct((s, N_HEADS * V_HEAD), jnp.bfloat16),
        scratch_shapes=[pltpu.VMEM((2, 1, tq), jnp.float32),
                        pltpu.VMEM((2, V_ROWS, tq), jnp.float32),
                        ] + [pltpu.VMEM((ATT_K, ATT_TILE), jnp.float32)] * S_BUFS,
        compiler_params=pltpu.CompilerParams(
            dimension_semantics=("arbitrary", "arbitrary"), vmem_limit_bytes=VMEM_LIMIT),
        name="mla_attention",
    )(q, k, vt)


def _post_kernel(x_ref, halo_ref, attn_ref, p_ref, g_pre_ref, wpool_in_ref, wgate_ref, bgate_ref,
                 wpool_ref, pscale_ref, wba_ref, wbp_ref, wout_ref, g_post_ref, g_mlp_ref,
                 wff1_ref, wff2_ref, g_pmlp_ref, wpe_ref, wpg_ref, g_ple_ref, o_ref, ext_ref):
    bf = jnp.bfloat16
    tm = x_ref.shape[0]
    ts = tm // POST_SPLIT
    i = pl.program_id(0)

    def rows_chain(part):
        r0 = part * ts
        rows = slice(r0, r0 + ts)
        x = x_ref[rows, :]
        a = _rms(x, g_pre_ref[...]).astype(bf)
        yield

        if part == 0:
            x_halo, have_halo = halo_ref[...], i > 0
        else:
            x_halo, have_halo = x_ref[r0 - POOL_HALO:r0, :], True
        u_halo = _dot(_rms(x_halo, g_pre_ref[...]).astype(bf), wpool_in_ref[...])
        u = _dot(a, wpool_in_ref[...])
        ext = ext_ref.at[part]
        ext[0:POOL_HALO, :] = jnp.where(have_halo, u_halo, 0.0)
        ext[POOL_HALO:, :] = u
        yield
        gates = jax.nn.sigmoid(_dot(a, wgate_ref[...]) + bgate_ref[...])
        attn_branch = gates[:, :D_MODEL] * _dot(attn_ref[rows, :], wba_ref[...])
        yield
        t = i * tm + r0 + lax.broadcasted_iota(jnp.int32, (ts, 1), 0)
        pooled = []
        for g, w in enumerate(POOL_WINDOWS):
            cols = slice(g * POOL_GROUP, (g + 1) * POOL_GROUP)
            wsum = ext[:, cols]
            shift = 1
            while shift < w:
                wsum = wsum + pltpu.roll(wsum, shift, 0)
                shift *= 2
            wsum = wsum[POOL_HALO:, :]
            cnt = jnp.minimum(t + 1, w).astype(jnp.float32)
            d = wsum / cnt - u[:, cols]
            pooled.append(_dot(d.astype(bf), wpool_ref[g]))
        pooled = jnp.concatenate(pooled, axis=1) * pscale_ref[...]
        yield

        merged = attn_branch + gates[:, D_MODEL:] * _dot(pooled.astype(bf), wbp_ref[...])
        yield
        y = _dot(merged.astype(bf), wout_ref[...])
        yield
        h = x + _rms(y, g_post_ref[...])

        m = _rms(h, g_mlp_ref[...]).astype(bf)
        yield
        f = jnp.zeros((ts, D_MODEL), jnp.float32)
        for c in range(D_FF // FF_CHUNK):
            cols = slice(c * FF_CHUNK, (c + 1) * FF_CHUNK)
            hid = jnp.square(jnp.maximum(_dot(m, wff1_ref[:, cols]), 0.0))
            f = f + _dot(hid.astype(bf), wff2_ref[cols, :])
            yield
        h = h + _rms(f, g_pmlp_ref[...])
        yield

        e = _dot(p_ref[rows, :].astype(bf), wpe_ref[...])
        pg = jax.nn.sigmoid(_dot(h.astype(bf), wpg_ref[...]))
        yield
        o_ref[rows, :] = h + _rms(pg * e, g_ple_ref[...])

    chains = [rows_chain(part) for part in range(POST_SPLIT)]
    while chains:
        chains = [c for c in chains if next(c, True) is None]


def _post_call(x, attn, p, g_pre, wpool_in, wgate, bgate, wpool, pscale, wba, wbp, wout,
               g_post, g_mlp, wff1, wff2, g_pmlp, wpe, wpg, g_ple):
    s = x.shape[0]
    tm = POST_ROWS
    row = lambda i: (i, 0)
    halo_blocks = tm // POOL_HALO

    def const(arr):
        zeros = (0,) * arr.ndim
        return pl.BlockSpec(arr.shape, lambda i: zeros, pipeline_mode=pl.Buffered(1))

    weights = (g_pre, wpool_in, wgate, bgate, wpool, pscale, wba, wbp, wout, g_post, g_mlp,
               wff1, wff2, g_pmlp, wpe, wpg, g_ple)
    return pl.pallas_call(
        _post_kernel,
        grid=(s // tm,),
        in_specs=[pl.BlockSpec((tm, D_MODEL), row),
                  pl.BlockSpec((POOL_HALO, D_MODEL),
                               lambda i: (jnp.maximum(i * halo_blocks - 1, 0), 0)),
                  pl.BlockSpec((tm, N_HEADS * V_HEAD), row),
                  pl.BlockSpec((tm, PLE_DIM), row)] + [const(w) for w in weights],
        out_specs=pl.BlockSpec((tm, D_MODEL), row),
        out_shape=jax.ShapeDtypeStruct((s, D_MODEL), jnp.float32),
        scratch_shapes=[pltpu.VMEM((POST_SPLIT, tm // POST_SPLIT + POOL_HALO, POOL_WIDTH),
                                   jnp.float32)],
        compiler_params=pltpu.CompilerParams(
            dimension_semantics=("arbitrary",), vmem_limit_bytes=VMEM_LIMIT),
        name="post_attention",
    )(x, x, attn, p, *weights)


def _pad_heads(w, per_head, lane0=0):
    kdim = w.shape[0]
    w = w.reshape(kdim, N_HEADS, per_head)
    w = jnp.pad(w, ((0, 0), (0, 0), (lane0, HEAD_PAD - per_head - lane0)))
    return w.reshape(kdim, N_HEADS * HEAD_PAD)


def _rope_tile(nope, rope):
    x1, x2 = rope[..., :ROPE_HALF], rope[..., ROPE_HALF:]
    return jnp.concatenate([nope, x1, x2, x2, x1], axis=-1)


def _layer(h, p, pos, invf, g_pre_mix, w_in, b_gate, g_q, w_uq, g_kv, w_ukv, w_pool, pool_scale,
           w_branch_attn, w_branch_pool, w_out, g_post_mix, g_pre_mlp, w_ff1, w_ff2, g_post_mlp,
           w_ple_proj, w_ple_gate, g_ple):
    bf = jnp.bfloat16
    row = lambda v: v.reshape(1, -1)
    o_kv = Q_LORA
    o_kr = o_kv + KV_LORA
    o_pool = o_kr + QK_ROPE
    o_gate = o_pool + POOL_WIDTH

    w_kr = _rope_tile(jnp.zeros((D_MODEL, QK_NOPE), w_in.dtype), w_in[:, o_kr:o_pool])
    w_in_qkv = jnp.concatenate([w_in[:, :o_kr], w_kr], axis=1).astype(bf)
    w_uq_h = w_uq.reshape(Q_LORA, N_HEADS, QK_NOPE + QK_ROPE)
    w_uq_p = _rope_tile(w_uq_h[..., :QK_NOPE], w_uq_h[..., QK_NOPE:]).reshape(Q_LORA, -1).astype(bf)
    w_ukv_h = w_ukv.reshape(KV_LORA, N_HEADS, QK_NOPE + V_HEAD)
    w_uk_p = _pad_heads(w_ukv_h[:, :, :QK_NOPE].reshape(KV_LORA, -1), QK_NOPE).astype(bf)
    w_uv_t = jnp.pad(jnp.transpose(w_ukv_h[:, :, QK_NOPE:], (1, 2, 0)),
                     ((0, 0), (0, V_ROWS - V_HEAD), (0, 0))).reshape(N_HEADS * V_ROWS, KV_LORA).astype(bf)

    q, k, vt = _qkv_call(h, pos, invf, row(g_pre_mix), w_in_qkv, row(g_q), w_uq_p, row(g_kv),
                         w_uk_p, w_uv_t)
    attn = _attn_call(q, k, vt)
    return _post_call(
        h, attn, p, row(g_pre_mix), w_in[:, o_pool:o_gate].astype(bf), w_in[:, o_gate:].astype(bf),
        row(b_gate), w_pool.astype(bf), row(pool_scale), w_branch_attn.astype(bf),
        w_branch_pool.astype(bf), w_out.astype(bf), row(g_post_mix), row(g_pre_mlp),
        w_ff1.astype(bf), w_ff2.astype(bf), row(g_post_mlp), w_ple_proj.astype(bf),
        w_ple_gate.astype(bf), row(g_ple))


@jax.jit
def kernel(x, p, positions, g_pre_mix, w_in, b_gate, g_q, w_uq, g_kv, w_ukv, w_pool, pool_scale,
           w_branch_attn, w_branch_pool, w_out, g_post_mix, g_pre_mlp, w_ff1, w_ff2, g_post_mlp,
           w_ple_proj, w_ple_gate, g_ple):
    batch, seq, _ = x.shape
    depth = w_in.shape[0]
    inv_freq = ROPE_THETA ** (-jnp.arange(0, QK_ROPE, 2, dtype=jnp.float32) / QK_ROPE)
    invf = jnp.tile(inv_freq, 2 * ROPE_PACK).reshape(1, LANES)
    ts = V_CHUNK // QKV_SPLIT
    outs = []
    for b in range(batch):
        h = x[b]
        pos = positions[b].reshape(seq // ts, ROPE_PACK, ts // ROPE_PACK).transpose(0, 2, 1)
        pos = jnp.repeat(pos.reshape(seq // ROPE_PACK, ROPE_PACK), QK_ROPE, axis=1)
        for l in range(depth):
            h = _layer(h, p[l, b], pos, invf, g_pre_mix[l], w_in[l], b_gate[l], g_q[l], w_uq[l],
                       g_kv[l], w_ukv[l], w_pool[l], pool_scale[l], w_branch_attn[l],
                       w_branch_pool[l], w_out[l], g_post_mix[l], g_pre_mlp[l], w_ff1[l], w_ff2[l],
                       g_post_mlp[l], w_ple_proj[l], w_ple_gate[l], g_ple[l])
        outs.append(h)
    return jnp.stack(outs, axis=0)
```

```python
import functools
import math
from typing import Any, NamedTuple, Optional

import jax
import jax.numpy as jnp
from jax import lax
from jax.experimental import pallas as pl
from jax.experimental.pallas import tpu as pltpu

D_MODEL = 1024
PLE_DIM = 256
N_HEADS = 8
QK_NOPE = 64
QK_ROPE = 32
V_HEAD = 64
Q_LORA = 384
KV_LORA = 256
POOL_WINDOWS = (2, 4, 8, 16)
POOL_GROUP = 128
POOL_WIDTH = POOL_GROUP * len(POOL_WINDOWS)
D_FF = 4 * D_MODEL
ROPE_THETA = 10000.0
EPS = 1e-6

LANES = 128
HEAD_PAD = LANES
ROPE_LANE0 = QK_NOPE
ROPE_HALF = QK_ROPE // 2
POOL_HALO = max(POOL_WINDOWS)
MASK_VALUE = -1e30

V_ROWS = 80
V_CHUNK = 512
QKV_SPLIT = 2
ROPE_PACK = LANES // QK_ROPE
ATT_Q = 4096
ATT_K = 2048
ATT_TILE = 256
MAX_CHAINS = 2
S_BUFS = 4
POST_ROWS = 512
POST_SPLIT = 2
FF_CHUNK = 1024
VMEM_LIMIT = 60 * 1024 * 1024

Q_SCALE = (QK_NOPE + QK_ROPE) ** -0.5 * math.log2(math.e)


def _rms(x, g):
    y = x * lax.rsqrt(jnp.mean(x * x, axis=-1, keepdims=True) + EPS)
    return y * g


def _dot(a, b):
    return jnp.dot(a, b, preferred_element_type=jnp.float32)


def _qkv_kernel(x_ref, pos_ref, invf_ref, g_ref, win_ref, gq_ref, wuq_ref, gkv_ref,
                wuk_ref, wuvt_ref, q_ref, k_ref, vt_ref):
    bf = jnp.bfloat16
    ts = x_ref.shape[0] // QKV_SPLIT
    tp = ts // ROPE_PACK

    def rows_chain(part):
        rows = slice(part * ts, (part + 1) * ts)
        a = _rms(x_ref[rows, :], g_ref[...]).astype(bf)
        yield
        proj = _dot(a, win_ref[...])
        yield
        q_n = _rms(proj[:, :Q_LORA], gq_ref[...]).astype(bf)
        kv_n = _rms(proj[:, Q_LORA:Q_LORA + KV_LORA], gkv_ref[...]).astype(bf)
        k_rope = proj[:, Q_LORA + KV_LORA:]
        yield
        q_all = _dot(q_n, wuq_ref[...])
        yield
        k_all = _dot(kv_n, wuk_ref[...])
        v_t = lax.dot_general(wuvt_ref[...], kv_n, (((1,), (1,)), ((), ())),
                              preferred_element_type=jnp.float32)
        v_row = lax.broadcasted_iota(jnp.int32, v_t.shape, 0) % V_ROWS
        vt_ref[0, :, rows] = jnp.where(v_row == V_HEAD, 1.0, v_t).astype(bf)
        yield

        ang = pos_ref[part * tp:(part + 1) * tp, :].astype(jnp.float32) * invf_ref[...]
        cos_p, sin_p = jnp.cos(ang), jnp.sin(ang)
        lane = lax.broadcasted_iota(jnp.int32, (tp, LANES), 1)
        in_x1 = (lane >= ROPE_LANE0) & (lane < ROPE_LANE0 + ROPE_HALF)
        in_x2 = (lane >= ROPE_LANE0 + ROPE_HALF) & (lane < ROPE_LANE0 + QK_ROPE)
        cos_t, sin_t = [], []
        for qr in range(ROPE_PACK):
            shift = (ROPE_LANE0 - QK_ROPE * qr) % LANES
            c = pltpu.roll(cos_p, shift, 1) if shift else cos_p
            s = pltpu.roll(sin_p, shift, 1) if shift else sin_p
            cos_t.append(jnp.where(lane < ROPE_LANE0, 1.0, jnp.where(in_x1 | in_x2, c, 0.0)))
            sin_t.append(jnp.where(in_x1, -s, jnp.where(in_x2, s, 0.0)))
        cos_t = jnp.concatenate(cos_t, axis=0)
        sin_t = jnp.concatenate(sin_t, axis=0)
        yield

        def rope(z):
            return z * cos_t + pltpu.roll(z, LANES - QK_ROPE, 1) * sin_t

        k_rope = rope(k_rope)
        for h in range(N_HEADS):
            sl = slice(h * HEAD_PAD, (h + 1) * HEAD_PAD)
            q_ref[rows, sl] = (rope(q_all[:, sl]) * Q_SCALE).astype(bf)
            k_ref[rows, sl] = (k_all[:, sl] + k_rope).astype(bf)
            if h % 2:
                yield

    chains = [rows_chain(part) for part in range(QKV_SPLIT)]
    while chains:
        chains = [c for c in chains if next(c, True) is None]


def _qkv_call(x, pos_packed, invf, g, win, gq, wuq, gkv, wuk, wuvt):
    s = x.shape[0]
    tm = V_CHUNK
    row = lambda i: (i, 0)
    fixed = lambda i: (0, 0)
    full = lambda arr: pl.BlockSpec(arr.shape, fixed)
    qk = jax.ShapeDtypeStruct((s, N_HEADS * HEAD_PAD), jnp.bfloat16)
    vt = jax.ShapeDtypeStruct((s // tm, N_HEADS * V_ROWS, tm), jnp.bfloat16)
    return pl.pallas_call(
        _qkv_kernel,
        grid=(s // tm,),
        in_specs=[pl.BlockSpec((tm, D_MODEL), row), pl.BlockSpec((tm // ROPE_PACK, LANES), row),
                  full(invf), full(g), full(win), full(gq), full(wuq), full(gkv), full(wuk),
                  full(wuvt)],
        out_specs=[pl.BlockSpec((tm, N_HEADS * HEAD_PAD), row),
                   pl.BlockSpec((tm, N_HEADS * HEAD_PAD), row),
                   pl.BlockSpec((1, N_HEADS * V_ROWS, tm), lambda i: (i, 0, 0))],
        out_shape=[qk, qk, vt],
        compiler_params=pltpu.CompilerParams(
            dimension_semantics=("arbitrary",), vmem_limit_bytes=VMEM_LIMIT),
        name="qkv_proj",
    )(x, pos_packed, invf, g, win, gq, wuq, gkv, wuk, wuvt)


class _Unit(NamedTuple):
    head: int
    col: int
    kv_start: Any
    width: int
    shift: Optional[int]


def _attn_kernel(q_ref, k_ref, vt_ref, o_ref, m_ref, acc_ref, *s_bufs, tk):
    tq = q_ref.shape[0]
    n_buf = len(s_bufs)
    ahead = n_buf - 1
    i = pl.program_id(1)
    m_ref[...] = jnp.full(m_ref.shape, MASK_VALUE, jnp.float32)
    acc_ref[...] = jnp.zeros(acc_ref.shape, jnp.float32)

    def scores(u, slot):
        sl = slice(u.head * HEAD_PAD, (u.head + 1) * HEAD_PAD)
        s = lax.dot_general(k_ref[pl.ds(u.kv_start, u.width), sl], q_ref[u.col:u.col + ATT_TILE, sl],
                            (((1,), (1,)), ((), ())), preferred_element_type=jnp.float32)
        if u.shift is not None:
            r = lax.broadcasted_iota(jnp.int32, s.shape, 0)
            c = lax.broadcasted_iota(jnp.int32, s.shape, 1)
            s = jnp.where(r <= c + u.shift, s, MASK_VALUE)
        s_bufs[slot][:u.width, :] = s
        rows = u.width // MAX_CHAINS
        parts = [jnp.max(s[g * rows:(g + 1) * rows], axis=0, keepdims=True)
                 for g in range(MAX_CHAINS)]
        while len(parts) > 1:
            parts = [jnp.maximum(a, b) for a, b in zip(parts[::2], parts[1::2])]
        return parts[0]

    def update(u, slot, col_max):
        qc = slice(u.col, u.col + ATT_TILE)
        m_prev = m_ref[u.head, :, qc]
        m_new = jnp.maximum(m_prev, col_max)
        alpha = jnp.exp2(m_prev - m_new)
        p = jnp.exp2(s_bufs[slot][:u.width, :] - m_new).astype(jnp.bfloat16)
        vrows = slice(u.head * V_ROWS, (u.head + 1) * V_ROWS)
        slab0 = u.kv_start // V_CHUNK
        pv = None
        for c, lo in enumerate(range(0, u.width, V_CHUNK)):
            n = min(V_CHUNK, u.width - lo)
            part = _dot(vt_ref[slab0 + c, vrows, :n], p[lo:lo + n])
            pv = part if pv is None else pv + part
        acc_ref[u.head, :, qc] = alpha * acc_ref[u.head, :, qc] + pv
        m_ref[u.head, :, qc] = m_new

    def run(units, ready=(), then=(), after=None):
        seq = list(units) + list(then)
        maxes = dict(enumerate(ready))
        for n in range(len(maxes), min(ahead, len(seq))):
            maxes[n] = scores(seq[n], n % n_buf)
        for n, u in enumerate(units):
            if n + ahead < len(seq):
                maxes[n + ahead] = scores(seq[n + ahead], (n + ahead) % n_buf)
            update(u, n % n_buf, maxes.pop(n))
            if after is not None:
                after(u)
        return tuple(maxes[len(units) + n] for n in range(len(then)))

    tile_order = list(range(tk, tq, ATT_TILE)) + list(range(0, tk, ATT_TILE))

    def full_units(block):
        kv_start = pl.multiple_of(block * tk, tk)
        return [_Unit(hh, c0, kv_start, tk, None) for hh in range(2) for c0 in tile_order]

    n_main = (i * tq) // tk
    assert tq == 2 * tk and ahead <= tk // ATT_TILE and (2 * (tq // ATT_TILE)) % n_buf == 0

    def body(j, first_maxes):
        return run(full_units(j), ready=first_maxes, then=full_units(j + 1)[:ahead])

    first = full_units(0)[:ahead]
    first_maxes = lax.fori_loop(0, n_main, body,
                                tuple(scores(u, n) for n, u in enumerate(first)))

    diag_start = pl.multiple_of(i * tq, tq)
    quarter = [u for u in full_units(n_main) if u.col >= tk]
    masked = []
    for c0 in range(tk - ATT_TILE, -ATT_TILE, -ATT_TILE):
        for hh in range(2):
            masked.append(_Unit(hh, c0, diag_start, c0 + ATT_TILE, c0))
        for hh in range(2):
            masked.append(_Unit(hh, c0 + tk, diag_start + tk, c0 + ATT_TILE, c0))

    def finish(u):
        if u.head != 1 or u.shift is None:
            return
        qc = slice(u.col, u.col + ATT_TILE)
        o_t = jnp.concatenate(
            [acc_ref[hh, :V_HEAD, qc] / acc_ref[hh, V_HEAD:V_HEAD + 1, qc] for hh in range(2)],
            axis=0)
        o_ref[qc, :] = o_t.T.astype(o_ref.dtype)

    run(quarter + masked, ready=first_maxes, after=finish)


def _attn_call(q, k, vt):
    s = q.shape[0]
    tq = ATT_Q
    pair = 2 * HEAD_PAD
    return pl.pallas_call(
        functools.partial(_attn_kernel, tk=ATT_K),
        grid=(N_HEADS // 2, s // tq),
        in_specs=[pl.BlockSpec((tq, pair), lambda hp, i: (i, hp)),
                  pl.BlockSpec((s, pair), lambda hp, i: (0, hp)),
                  pl.BlockSpec((s // V_CHUNK, 2 * V_ROWS, V_CHUNK), lambda hp, i: (0, hp, 0))],
        out_specs=pl.BlockSpec((tq, 2 * V_HEAD), lambda hp, i: (i, hp)),
        out_shape=jax.ShapeDtypeStruct((s, N_HEADS * V_HEAD), jnp.bfloat16),
        scratch_shapes=[pltpu.VMEM((2, 1, tq), jnp.float32),
                        pltpu.VMEM((2, V_ROWS, tq), jnp.float32),
                        ] + [pltpu.VMEM((ATT_K, ATT_TILE), jnp.float32)] * S_BUFS,
        compiler_params=pltpu.CompilerParams(
            dimension_semantics=("arbitrary", "arbitrary"), vmem_limit_bytes=VMEM_LIMIT),
        name="mla_attention",
    )(q, k, vt)


def _post_kernel(x_ref, halo_ref, attn_ref, p_ref, g_pre_ref, wpool_in_ref, wgate_ref, bgate_ref,
                 wpool_ref, pscale_ref, wba_ref, wbp_ref, wout_ref, g_post_ref, g_mlp_ref,
                 wff1_ref, wff2_ref, g_pmlp_ref, wpe_ref, wpg_ref, g_ple_ref, o_ref, ext_ref):
    bf = jnp.bfloat16
    tm = x_ref.shape[0]
    ts = tm // POST_SPLIT
    i = pl.program_id(0)

    def rows_chain(part):
        r0 = part * ts
        rows = slice(r0, r0 + ts)
        x = x_ref[rows, :]
        a = _rms(x, g_pre_ref[...]).astype(bf)
        yield

        if part == 0:
            x_halo, have_halo = halo_ref[...], i > 0
        else:
            x_halo, have_halo = x_ref[r0 - POOL_HALO:r0, :], True
        u_halo = _dot(_rms(x_halo, g_pre_ref[...]).astype(bf), wpool_in_ref[...])
        u = _dot(a, wpool_in_ref[...])
        ext = ext_ref.at[part]
        ext[0:POOL_HALO, :] = jnp.where(have_halo, u_halo, 0.0)
        ext[POOL_HALO:, :] = u
        yield
        gates = jax.nn.sigmoid(_dot(a, wgate_ref[...]) + bgate_ref[...])
        attn_branch = gates[:, :D_MODEL] * _dot(attn_ref[rows, :], wba_ref[...])
        yield
        t = i * tm + r0 + lax.broadcasted_iota(jnp.int32, (ts, 1), 0)
        pooled = []
        for g, w in enumerate(POOL_WINDOWS):
            cols = slice(g * POOL_GROUP, (g + 1) * POOL_GROUP)
            wsum = ext[:, cols]
            shift = 1
            while shift < w:
                wsum = wsum + pltpu.roll(wsum, shift, 0)
                shift *= 2
            wsum = wsum[POOL_HALO:, :]
            cnt = jnp.minimum(t + 1, w).astype(jnp.float32)
            d = wsum / cnt - u[:, cols]
            pooled.append(_dot(d.astype(bf), wpool_ref[g]))
        pooled = jnp.concatenate(pooled, axis=1) * pscale_ref[...]
        yield

        merged = attn_branch + gates[:, D_MODEL:] * _dot(pooled.astype(bf), wbp_ref[...])
        yield
        y = _dot(merged.astype(bf), wout_ref[...])
        yield
        h = x + _rms(y, g_post_ref[...])

        m = _rms(h, g_mlp_ref[...]).astype(bf)
        yield
        f = jnp.zeros((ts, D_MODEL), jnp.float32)
        for c in range(D_FF // FF_CHUNK):
            cols = slice(c * FF_CHUNK, (c + 1) * FF_CHUNK)
            hid = jnp.square(jnp.maximum(_dot(m, wff1_ref[:, cols]), 0.0))
            f = f + _dot(hid.astype(bf), wff2_ref[cols, :])
            yield
        h = h + _rms(f, g_pmlp_ref[...])
        yield

        e = _dot(p_ref[rows, :].astype(bf), wpe_ref[...])
        pg = jax.nn.sigmoid(_dot(h.astype(bf), wpg_ref[...]))
        yield
        o_ref[rows, :] = h + _rms(pg * e, g_ple_ref[...])

    chains = [rows_chain(part) for part in range(POST_SPLIT)]
    while chains:
        chains = [c for c in chains if next(c, True) is None]


def _post_call(x, attn, p, g_pre, wpool_in, wgate, bgate, wpool, pscale, wba, wbp, wout,
               g_post, g_mlp, wff1, wff2, g_pmlp, wpe, wpg, g_ple):
    s = x.shape[0]
    tm = POST_ROWS
    row = lambda i: (i, 0)
    halo_blocks = tm // POOL_HALO

    def const(arr):
        zeros = (0,) * arr.ndim
        return pl.BlockSpec(arr.shape, lambda i: zeros, pipeline_mode=pl.Buffered(1))

    weights = (g_pre, wpool_in, wgate, bgate, wpool, pscale, wba, wbp, wout, g_post, g_mlp,
               wff1, wff2, g_pmlp, wpe, wpg, g_ple)
    return pl.pallas_call(
        _post_kernel,
        grid=(s // tm,),
        in_specs=[pl.BlockSpec((tm, D_MODEL), row),
                  pl.BlockSpec((POOL_HALO, D_MODEL),
                               lambda i: (jnp.maximum(i * halo_blocks - 1, 0), 0)),
                  pl.BlockSpec((tm, N_HEADS * V_HEAD), row),
                  pl.BlockSpec((tm, PLE_DIM), row)] + [const(w) for w in weights],
        out_specs=pl.BlockSpec((tm, D_MODEL), row),
        out_shape=jax.ShapeDtypeStruct((s, D_MODEL), jnp.float32),
        scratch_shapes=[pltpu.VMEM((POST_SPLIT, tm // POST_SPLIT + POOL_HALO, POOL_WIDTH),
                                   jnp.float32)],
        compiler_params=pltpu.CompilerParams(
            dimension_semantics=("arbitrary",), vmem_limit_bytes=VMEM_LIMIT),
        name="post_attention",
    )(x, x, attn, p, *weights)


def _pad_heads(w, per_head):
    kdim = w.shape[0]
    w = w.reshape(kdim, N_HEADS, per_head)
    w = jnp.pad(w, ((0, 0), (0, 0), (0, HEAD_PAD - per_head)))
    return w.reshape(kdim, N_HEADS * HEAD_PAD)


def _rope_tile(nope, rope):
    x1, x2 = rope[..., :ROPE_HALF], rope[..., ROPE_HALF:]
    return jnp.concatenate([nope, x1, x2, x2, x1], axis=-1)


def _layer(h, p, pos, invf, g_pre_mix, w_in, b_gate, g_q, w_uq, g_kv, w_ukv, w_pool, pool_scale,
           w_branch_attn, w_branch_pool, w_out, g_post_mix, g_pre_mlp, w_ff1, w_ff2, g_post_mlp,
           w_ple_proj, w_ple_gate, g_ple):
    bf = jnp.bfloat16
    row = lambda v: v.reshape(1, -1)
    o_kv = Q_LORA
    o_kr = o_kv + KV_LORA
    o_pool = o_kr + QK_ROPE
    o_gate = o_pool + POOL_WIDTH

    w_kr = _rope_tile(jnp.zeros((D_MODEL, QK_NOPE), w_in.dtype), w_in[:, o_kr:o_pool])
    w_in_qkv = jnp.concatenate([w_in[:, :o_kr], w_kr], axis=1).astype(bf)
    w_uq_h = w_uq.reshape(Q_LORA, N_HEADS, QK_NOPE + QK_ROPE)
    w_uq_p = _rope_tile(w_uq_h[..., :QK_NOPE], w_uq_h[..., QK_NOPE:]).reshape(Q_LORA, -1).astype(bf)
    w_ukv_h = w_ukv.reshape(KV_LORA, N_HEADS, QK_NOPE + V_HEAD)
    w_uk_p = _pad_heads(w_ukv_h[:, :, :QK_NOPE].reshape(KV_LORA, -1), QK_NOPE).astype(bf)
    w_uv_t = jnp.pad(jnp.transpose(w_ukv_h[:, :, QK_NOPE:], (1, 2, 0)),
                     ((0, 0), (0, V_ROWS - V_HEAD), (0, 0))).reshape(N_HEADS * V_ROWS, KV_LORA).astype(bf)

    q, k, vt = _qkv_call(h, pos, invf, row(g_pre_mix), w_in_qkv, row(g_q), w_uq_p, row(g_kv),
                         w_uk_p, w_uv_t)
    attn = _attn_call(q, k, vt)
    return _post_call(
        h, attn, p, row(g_pre_mix), w_in[:, o_pool:o_gate].astype(bf), w_in[:, o_gate:].astype(bf),
        row(b_gate), w_pool.astype(bf), row(pool_scale), w_branch_attn.astype(bf),
        w_branch_pool.astype(bf), w_out.astype(bf), row(g_post_mix), row(g_pre_mlp),
        w_ff1.astype(bf), w_ff2.astype(bf), row(g_post_mlp), w_ple_proj.astype(bf),
        w_ple_gate.astype(bf), row(g_ple))


@jax.jit
def kernel(x, p, positions, g_pre_mix, w_in, b_gate, g_q, w_uq, g_kv, w_ukv, w_pool, pool_scale,
           w_branch_attn, w_branch_pool, w_out, g_post_mix, g_pre_mlp, w_ff1, w_ff2, g_post_mlp,
           w_ple_proj, w_ple_gate, g_ple):
    batch, seq, _ = x.shape
    depth = w_in.shape[0]
    inv_freq = ROPE_THETA ** (-jnp.arange(0, QK_ROPE, 2, dtype=jnp.float32) / QK_ROPE)
    invf = jnp.tile(inv_freq, 2 * ROPE_PACK).reshape(1, LANES)
    ts = V_CHUNK // QKV_SPLIT
    outs = []
    for b in range(batch):
        h = x[b]
        pos = positions[b].reshape(seq // ts, ROPE_PACK, ts // ROPE_PACK).transpose(0, 2, 1)
        pos = jnp.repeat(pos.reshape(seq // ROPE_PACK, ROPE_PACK), QK_ROPE, axis=1)
        for l in range(depth):
            h = _layer(h, p[l, b], pos, invf, g_pre_mix[l], w_in[l], b_gate[l], g_q[l], w_uq[l],
                       g_kv[l], w_ukv[l], w_pool[l], pool_scale[l], w_branch_attn[l],
                       w_branch_pool[l], w_out[l], g_post_mix[l], g_pre_mlp[l], w_ff1[l], w_ff2[l],
                       g_post_mlp[l], w_ple_proj[l], w_ple_gate[l], g_ple[l])
        outs.append(h)
    return jnp.stack(outs, axis=0)
```

```python
import functools
import math
from typing import Any, NamedTuple, Optional

import jax
import jax.numpy as jnp
from jax import lax
from jax.experimental import pallas as pl
from jax.experimental.pallas import tpu as pltpu

D_MODEL = 1024
PLE_DIM = 256
N_HEADS = 8
QK_NOPE = 64
QK_ROPE = 32
V_HEAD = 64
Q_LORA = 384
KV_LORA = 256
POOL_WINDOWS = (2, 4, 8, 16)
POOL_GROUP = 128
POOL_WIDTH = POOL_GROUP * len(POOL_WINDOWS)
D_FF = 4 * D_MODEL
ROPE_THETA = 10000.0
EPS = 1e-6

LANES = 128
HEAD_PAD = LANES
ROPE_LANE0 = QK_NOPE
ROPE_HALF = QK_ROPE // 2
POOL_HALO = max(POOL_WINDOWS)
MASK_VALUE = float("-inf")

V_ROWS = 80
V_CHUNK = 512
QKV_SPLIT = 2
ROPE_PACK = LANES // QK_ROPE
ATT_Q = 4096
ATT_K = 2048
ATT_TILE = 256
MAX_CHAINS = 2
S_BUFS = 4
POST_ROWS = 512
POST_SPLIT = 2
FF_CHUNK = 1024
VMEM_LIMIT = 60 * 1024 * 1024

Q_SCALE = (QK_NOPE + QK_ROPE) ** -0.5 * math.log2(math.e)


def _rms(x, g):
    y = x * lax.rsqrt(jnp.mean(x * x, axis=-1, keepdims=True) + EPS)
    return y * g


def _dot(a, b):
    return jnp.dot(a, b, preferred_element_type=jnp.float32)


def _qkv_kernel(x_ref, pos_ref, invf_ref, g_ref, win_ref, gq_ref, wuq_ref, gkv_ref,
                wuk_ref, wuvt_ref, q_ref, k_ref, vt_ref):
    bf = jnp.bfloat16
    ts = x_ref.shape[0] // QKV_SPLIT
    tp = ts // ROPE_PACK

    def rows_chain(part):
        rows = slice(part * ts, (part + 1) * ts)
        a = _rms(x_ref[rows, :], g_ref[...]).astype(bf)
        yield
        proj = _dot(a, win_ref[...])
        yield
        q_n = _rms(proj[:, :Q_LORA], gq_ref[...]).astype(bf)
        kv_n = _rms(proj[:, Q_LORA:Q_LORA + KV_LORA], gkv_ref[...]).astype(bf)
        k_rope = proj[:, Q_LORA + KV_LORA:]
        yield
        q_all = _dot(q_n, wuq_ref[...])
        yield
        k_all = _dot(kv_n, wuk_ref[...])
        v_t = lax.dot_general(wuvt_ref[...], kv_n, (((1,), (1,)), ((), ())),
                              preferred_element_type=jnp.float32)
        v_row = lax.broadcasted_iota(jnp.int32, v_t.shape, 0) % V_ROWS
        vt_ref[0, :, rows] = jnp.where(v_row == V_HEAD, 1.0, v_t).astype(bf)
        yield

        ang = pos_ref[part * tp:(part + 1) * tp, :].astype(jnp.float32) * invf_ref[...]
        cos_p, sin_p = jnp.cos(ang), jnp.sin(ang)
        lane = lax.broadcasted_iota(jnp.int32, (tp, LANES), 1)
        in_x1 = (lane >= ROPE_LANE0) & (lane < ROPE_LANE0 + ROPE_HALF)
        in_x2 = (lane >= ROPE_LANE0 + ROPE_HALF) & (lane < ROPE_LANE0 + QK_ROPE)
        cos_t, sin_t = [], []
        for qr in range(ROPE_PACK):
            shift = (ROPE_LANE0 - QK_ROPE * qr) % LANES
            c = pltpu.roll(cos_p, shift, 1) if shift else cos_p
            s = pltpu.roll(sin_p, shift, 1) if shift else sin_p
            cos_t.append(jnp.where(lane < ROPE_LANE0, 1.0, jnp.where(in_x1 | in_x2, c, 0.0)))
            sin_t.append(jnp.where(in_x1, -s, jnp.where(in_x2, s, 0.0)))
        cos_t = jnp.concatenate(cos_t, axis=0)
        sin_t = jnp.concatenate(sin_t, axis=0)
        yield

        def rope(z):
            return z * cos_t + pltpu.roll(z, LANES - QK_ROPE, 1) * sin_t

        k_rope = rope(k_rope)
        for h in range(N_HEADS):
            sl = slice(h * HEAD_PAD, (h + 1) * HEAD_PAD)
            q_ref[rows, sl] = (rope(q_all[:, sl]) * Q_SCALE).astype(bf)
            k_ref[rows, sl] = (k_all[:, sl] + k_rope).astype(bf)
            if h % 2:
                yield

    chains = [rows_chain(part) for part in range(QKV_SPLIT)]
    while chains:
        chains = [c for c in chains if next(c, True) is None]


def _qkv_call(x, pos_packed, invf, g, win, gq, wuq, gkv, wuk, wuvt):
    s = x.shape[0]
    tm = V_CHUNK
    row = lambda i: (i, 0)
    fixed = lambda i: (0, 0)
    full = lambda arr: pl.BlockSpec(arr.shape, fixed)
    qk = jax.ShapeDtypeStruct((s, N_HEADS * HEAD_PAD), jnp.bfloat16)
    vt = jax.ShapeDtypeStruct((s // tm, N_HEADS * V_ROWS, tm), jnp.bfloat16)
    return pl.pallas_call(
        _qkv_kernel,
        grid=(s // tm,),
        in_specs=[pl.BlockSpec((tm, D_MODEL), row), pl.BlockSpec((tm // ROPE_PACK, LANES), row),
                  full(invf), full(g), full(win), full(gq), full(wuq), full(gkv), full(wuk),
                  full(wuvt)],
        out_specs=[pl.BlockSpec((tm, N_HEADS * HEAD_PAD), row),
                   pl.BlockSpec((tm, N_HEADS * HEAD_PAD), row),
                   pl.BlockSpec((1, N_HEADS * V_ROWS, tm), lambda i: (i, 0, 0))],
        out_shape=[qk, qk, vt],
        compiler_params=pltpu.CompilerParams(
            dimension_semantics=("arbitrary",), vmem_limit_bytes=VMEM_LIMIT),
        name="qkv_proj",
    )(x, pos_packed, invf, g, win, gq, wuq, gkv, wuk, wuvt)


class _Unit(NamedTuple):
    head: int
    col: int
    kv_start: Any
    width: int
    shift: Optional[int]


def _attn_kernel(q_ref, k_ref, vt_ref, o_ref, m_ref, acc_ref, *s_bufs, tk):
    tq = q_ref.shape[0]
    n_buf = len(s_bufs)
    ahead = n_buf - 1
    i = pl.program_id(1)
    m_ref[...] = jnp.full(m_ref.shape, MASK_VALUE, jnp.float32)
    acc_ref[...] = jnp.zeros(acc_ref.shape, jnp.float32)

    def scores(u, slot):
        sl = slice(u.head * HEAD_PAD, (u.head + 1) * HEAD_PAD)
        s = lax.dot_general(k_ref[pl.ds(u.kv_start, u.width), sl], q_ref[u.col:u.col + ATT_TILE, sl],
                            (((1,), (1,)), ((), ())), preferred_element_type=jnp.float32)
        if u.shift is not None:
            r = lax.broadcasted_iota(jnp.int32, s.shape, 0)
            c = lax.broadcasted_iota(jnp.int32, s.shape, 1)
            s = jnp.where(r <= c + u.shift, s, MASK_VALUE)
        s_bufs[slot][:u.width, :] = s
        rows = u.width // MAX_CHAINS
        parts = [jnp.max(s[g * rows:(g + 1) * rows], axis=0, keepdims=True)
                 for g in range(MAX_CHAINS)]
        while len(parts) > 1:
            parts = [jnp.maximum(a, b) for a, b in zip(parts[::2], parts[1::2])]
        return parts[0]

    def update(u, slot, col_max):
        qc = slice(u.col, u.col + ATT_TILE)
        m_prev = m_ref[u.head, :, qc]
        m_new = jnp.maximum(m_prev, col_max)
        alpha = jnp.exp2(m_prev - m_new)
        p = jnp.exp2(s_bufs[slot][:u.width, :] - m_new).astype(jnp.bfloat16)
        vrows = slice(u.head * V_ROWS, (u.head + 1) * V_ROWS)
        slab0 = u.kv_start // V_CHUNK
        pv = None
        for c, lo in enumerate(range(0, u.width, V_CHUNK)):
            n = min(V_CHUNK, u.width - lo)
            part = _dot(vt_ref[slab0 + c, vrows, :n], p[lo:lo + n])
            pv = part if pv is None else pv + part
        acc_ref[u.head, :, qc] = alpha * acc_ref[u.head, :, qc] + pv
        m_ref[u.head, :, qc] = m_new

    def run(units, ready=(), then=(), after=None):
        seq = list(units) + list(then)
        maxes = dict(enumerate(ready))
        for n in range(len(maxes), min(ahead, len(seq))):
            maxes[n] = scores(seq[n], n % n_buf)
        for n, u in enumerate(units):
            if n + ahead < len(seq):
                maxes[n + ahead] = scores(seq[n + ahead], (n + ahead) % n_buf)
            update(u, n % n_buf, maxes.pop(n))
            if after is not None:
                after(u)
        return tuple(maxes[len(units) + n] for n in range(len(then)))

    tile_order = list(range(tk, tq, ATT_TILE)) + list(range(0, tk, ATT_TILE))

    def full_units(block):
        kv_start = pl.multiple_of(block * tk, tk)
        return [_Unit(hh, c0, kv_start, tk, None) for hh in range(2) for c0 in tile_order]

    n_main = (i * tq) // tk
    assert tq == 2 * tk and ahead <= tk // ATT_TILE and (2 * (tq // ATT_TILE)) % n_buf == 0

    def body(j, first_maxes):
        return run(full_units(j), ready=first_maxes, then=full_units(j + 1)[:ahead])

    first = full_units(0)[:ahead]
    first_maxes = lax.fori_loop(0, n_main, body,
                                tuple(scores(u, n) for n, u in enumerate(first)))

    diag_start = pl.multiple_of(i * tq, tq)
    quarter = [u for u in full_units(n_main) if u.col >= tk]
    masked = []
    for c0 in range(tk - ATT_TILE, -ATT_TILE, -ATT_TILE):
        for hh in range(2):
            masked.append(_Unit(hh, c0, diag_start, c0 + ATT_TILE, c0))
        for hh in range(2):
            masked.append(_Unit(hh, c0 + tk, diag_start + tk, c0 + ATT_TILE, c0))

    def finish(u):
        if u.head != 1 or u.shift is None:
            return
        qc = slice(u.col, u.col + ATT_TILE)
        o_t = jnp.concatenate(
            [acc_ref[hh, :V_HEAD, qc] / acc_ref[hh, V_HEAD:V_HEAD + 1, qc] for hh in range(2)],
            axis=0)
        o_ref[qc, :] = o_t.T.astype(o_ref.dtype)

    run(quarter + masked, ready=first_maxes, after=finish)


def _attn_call(q, k, vt):
    s = q.shape[0]
    tq = ATT_Q
    pair = 2 * HEAD_PAD
    return pl.pallas_call(
        functools.partial(_attn_kernel, tk=ATT_K),
        grid=(N_HEADS // 2, s // tq),
        in_specs=[pl.BlockSpec((tq, pair), lambda hp, i: (i, hp)),
                  pl.BlockSpec((s, pair), lambda hp, i: (0, hp)),
                  pl.BlockSpec((s // V_CHUNK, 2 * V_ROWS, V_CHUNK), lambda hp, i: (0, hp, 0))],
        out_specs=pl.BlockSpec((tq, 2 * V_HEAD), lambda hp, i: (i, hp)),
        out_shape=jax.ShapeDtypeStruct((s, N_HEADS * V_HEAD), jnp.bfloat16),
        scratch_shapes=[pltpu.VMEM((2, 1, tq), jnp.float32),
                        pltpu.VMEM((2, V_ROWS, tq), jnp.float32),
                        ] + [pltpu.VMEM((ATT_K, ATT_TILE), jnp.float32)] * S_BUFS,
        compiler_params=pltpu.CompilerParams(
            dimension_semantics=("arbitrary", "arbitrary"), vmem_limit_bytes=VMEM_LIMIT),
        name="mla_attention",
    )(q, k, vt)


def _post_kernel(x_ref, halo_ref, attn_ref, p_ref, g_pre_ref, wpool_in_ref, wgate_ref, bgate_ref,
                 wpool_ref, pscale_ref, wba_ref, wbp_ref, wout_ref, g_post_ref, g_mlp_ref,
                 wff1_ref, wff2_ref, g_pmlp_ref, wpe_ref, wpg_ref, g_ple_ref, o_ref, ext_ref):
    bf = jnp.bfloat16
    tm = x_ref.shape[0]
    ts = tm // POST_SPLIT
    i = pl.program_id(0)

    def rows_chain(part):
        r0 = part * ts
        rows = slice(r0, r0 + ts)
        x = x_ref[rows, :]
        a = _rms(x, g_pre_ref[...]).astype(bf)
        yield

        if part == 0:
            x_halo, have_halo = halo_ref[...], i > 0
        else:
            x_halo, have_halo = x_ref[r0 - POOL_HALO:r0, :], True
        u_halo = _dot(_rms(x_halo, g_pre_ref[...]).astype(bf), wpool_in_ref[...])
        u = _dot(a, wpool_in_ref[...])
        ext = ext_ref.at[part]
        ext[0:POOL_HALO, :] = jnp.where(have_halo, u_halo, 0.0)
        ext[POOL_HALO:, :] = u
        yield
        gates = jax.nn.sigmoid(_dot(a, wgate_ref[...]) + bgate_ref[...])
        attn_branch = gates[:, :D_MODEL] * _dot(attn_ref[rows, :], wba_ref[...])
        yield
        t = i * tm + r0 + lax.broadcasted_iota(jnp.int32, (ts, 1), 0)
        pooled = []
        for g, w in enumerate(POOL_WINDOWS):
            cols = slice(g * POOL_GROUP, (g + 1) * POOL_GROUP)
            wsum = ext[:, cols]
            shift = 1
            while shift < w:
                wsum = wsum + pltpu.roll(wsum, shift, 0)
                shift *= 2
            wsum = wsum[POOL_HALO:, :]
            cnt = jnp.minimum(t + 1, w).astype(jnp.float32)
            d = wsum / cnt - u[:, cols]
            pooled.append(_dot(d.astype(bf), wpool_ref[g]))
        pooled = jnp.concatenate(pooled, axis=1) * pscale_ref[...]
        yield

        merged = attn_branch + gates[:, D_MODEL:] * _dot(pooled.astype(bf), wbp_ref[...])
        yield
        y = _dot(merged.astype(bf), wout_ref[...])
        yield
        h = x + _rms(y, g_post_ref[...])

        m = _rms(h, g_mlp_ref[...]).astype(bf)
        yield
        f = jnp.zeros((ts, D_MODEL), jnp.float32)
        for c in range(D_FF // FF_CHUNK):
            cols = slice(c * FF_CHUNK, (c + 1) * FF_CHUNK)
            hid = jnp.square(jnp.maximum(_dot(m, wff1_ref[:, cols]), 0.0))
            f = f + _dot(hid.astype(bf), wff2_ref[cols, :])
            yield
        h = h + _rms(f, g_pmlp_ref[...])
        yield

        e = _dot(p_ref[rows, :].astype(bf), wpe_ref[...])
        pg = jax.nn.sigmoid(_dot(h.astype(bf), wpg_ref[...]))
        yield
        o_ref[rows, :] = h + _rms(pg * e, g_ple_ref[...])

    chains = [rows_chain(part) for part in range(POST_SPLIT)]
    while chains:
        chains = [c for c in chains if next(c, True) is None]


def _post_call(x, attn, p, g_pre, wpool_in, wgate, bgate, wpool, pscale, wba, wbp, wout,
               g_post, g_mlp, wff1, wff2, g_pmlp, wpe, wpg, g_ple):
    s = x.shape[0]
    tm = POST_ROWS
    row = lambda i: (i, 0)
    halo_blocks = tm // POOL_HALO

    def const(arr):
        zeros = (0,) * arr.ndim
        return pl.BlockSpec(arr.shape, lambda i: zeros, pipeline_mode=pl.Buffered(1))

    weights = (g_pre, wpool_in, wgate, bgate, wpool, pscale, wba, wbp, wout, g_post, g_mlp,
               wff1, wff2, g_pmlp, wpe, wpg, g_ple)
    return pl.pallas_call(
        _post_kernel,
        grid=(s // tm,),
        in_specs=[pl.BlockSpec((tm, D_MODEL), row),
                  pl.BlockSpec((POOL_HALO, D_MODEL),
                               lambda i: (jnp.maximum(i * halo_blocks - 1, 0), 0)),
                  pl.BlockSpec((tm, N_HEADS * V_HEAD), row),
                  pl.BlockSpec((tm, PLE_DIM), row)] + [const(w) for w in weights],
        out_specs=pl.BlockSpec((tm, D_MODEL), row),
        out_shape=jax.ShapeDtypeStruct((s, D_MODEL), jnp.float32),
        scratch_shapes=[pltpu.VMEM((POST_SPLIT, tm // POST_SPLIT + POOL_HALO, POOL_WIDTH),
                                   jnp.float32)],
        compiler_params=pltpu.CompilerParams(
            dimension_semantics=("arbitrary",), vmem_limit_bytes=VMEM_LIMIT),
        name="post_attention",
    )(x, x, attn, p, *weights)


def _pad_heads(w, per_head):
    kdim = w.shape[0]
    w = w.reshape(kdim, N_HEADS, per_head)
    w = jnp.pad(w, ((0, 0), (0, 0), (0, HEAD_PAD - per_head)))
    return w.reshape(kdim, N_HEADS * HEAD_PAD)


def _rope_tile(nope, rope):
    x1, x2 = rope[..., :ROPE_HALF], rope[..., ROPE_HALF:]
    return jnp.concatenate([nope, x1, x2, x2, x1], axis=-1)


def _layer(h, p, pos, invf, g_pre_mix, w_in, b_gate, g_q, w_uq, g_kv, w_ukv, w_pool, pool_scale,
           w_branch_attn, w_branch_pool, w_out, g_post_mix, g_pre_mlp, w_ff1, w_ff2, g_post_mlp,
           w_ple_proj, w_ple_gate, g_ple):
    bf = jnp.bfloat16
    row = lambda v: v.reshape(1, -1)
    o_kv = Q_LORA
    o_kr = o_kv + KV_LORA
    o_pool = o_kr + QK_ROPE
    o_gate = o_pool + POOL_WIDTH

    w_kr = _rope_tile(jnp.zeros((D_MODEL, QK_NOPE), w_in.dtype), w_in[:, o_kr:o_pool])
    w_in_qkv = jnp.concatenate([w_in[:, :o_kr], w_kr], axis=1).astype(bf)
    w_uq_h = w_uq.reshape(Q_LORA, N_HEADS, QK_NOPE + QK_ROPE)
    w_uq_p = _rope_tile(w_uq_h[..., :QK_NOPE], w_uq_h[..., QK_NOPE:]).reshape(Q_LORA, -1).astype(bf)
    w_ukv_h = w_ukv.reshape(KV_LORA, N_HEADS, QK_NOPE + V_HEAD)
    w_uk_p = _pad_heads(w_ukv_h[:, :, :QK_NOPE].reshape(KV_LORA, -1), QK_NOPE).astype(bf)
    w_uv_t = jnp.pad(jnp.transpose(w_ukv_h[:, :, QK_NOPE:], (1, 2, 0)),
                     ((0, 0), (0, V_ROWS - V_HEAD), (0, 0))).reshape(N_HEADS * V_ROWS, KV_LORA).astype(bf)

    q, k, vt = _qkv_call(h, pos, invf, row(g_pre_mix), w_in_qkv, row(g_q), w_uq_p, row(g_kv),
                         w_uk_p, w_uv_t)
    attn = _attn_call(q, k, vt)
    return _post_call(
        h, attn, p, row(g_pre_mix), w_in[:, o_pool:o_gate].astype(bf), w_in[:, o_gate:].astype(bf),
        row(b_gate), w_pool.astype(bf), row(pool_scale), w_branch_attn.astype(bf),
        w_branch_pool.astype(bf), w_out.astype(bf), row(g_post_mix), row(g_pre_mlp),
        w_ff1.astype(bf), w_ff2.astype(bf), row(g_post_mlp), w_ple_proj.astype(bf),
        w_ple_gate.astype(bf), row(g_ple))


@jax.jit
def kernel(x, p, positions, g_pre_mix, w_in, b_gate, g_q, w_uq, g_kv, w_ukv, w_pool, pool_scale,
           w_branch_attn, w_branch_pool, w_out, g_post_mix, g_pre_mlp, w_ff1, w_ff2, g_post_mlp,
           w_ple_proj, w_ple_gate, g_ple):
    batch, seq, _ = x.shape
    depth = w_in.shape[0]
    inv_freq = ROPE_THETA ** (-jnp.arange(0, QK_ROPE, 2, dtype=jnp.float32) / QK_ROPE)
    invf = jnp.tile(inv_freq, 2 * ROPE_PACK).reshape(1, LANES)
    ts = V_CHUNK // QKV_SPLIT
    outs = []
    for b in range(batch):
        h = x[b]
        pos = positions[b].reshape(seq // ts, ROPE_PACK, ts // ROPE_PACK).transpose(0, 2, 1)
        pos = jnp.repeat(pos.reshape(seq // ROPE_PACK, ROPE_PACK), QK_ROPE, axis=1)
        for l in range(depth):
            h = _layer(h, p[l, b], pos, invf, g_pre_mix[l], w_in[l], b_gate[l], g_q[l], w_uq[l],
                       g_kv[l], w_ukv[l], w_pool[l], pool_scale[l], w_branch_attn[l],
                       w_branch_pool[l], w_out[l], g_post_mix[l], g_pre_mlp[l], w_ff1[l], w_ff2[l],
                       g_post_mlp[l], w_ple_proj[l], w_ple_gate[l], g_ple[l])
        outs.append(h)
    return jnp.stack(outs, axis=0)
```

```python
import functools
import math
from typing import Any, NamedTuple, Optional

import jax
import jax.numpy as jnp
from jax import lax
from jax.experimental import pallas as pl
from jax.experimental.pallas import tpu as pltpu

D_MODEL = 1024
PLE_DIM = 256
N_HEADS = 8
QK_NOPE = 64
QK_ROPE = 32
V_HEAD = 64
Q_LORA = 384
KV_LORA = 256
POOL_WINDOWS = (2, 4, 8, 16)
POOL_GROUP = 128
POOL_WIDTH = POOL_GROUP * len(POOL_WINDOWS)
D_FF = 4 * D_MODEL
ROPE_THETA = 10000.0
EPS = 1e-6

LANES = 128
HEAD_PAD = LANES
ROPE_LANE0 = QK_NOPE
ROPE_HALF = QK_ROPE // 2
POOL_HALO = max(POOL_WINDOWS)
MASK_VALUE = float("-inf")

V_ROWS = 80
V_CHUNK = 512
QKV_SPLIT = 2
ROPE_PACK = LANES // QK_ROPE
ATT_Q = 4096
ATT_K = 2048
ATT_TILE = 256
MAX_CHAINS = 2
S_BUFS = 4
POST_ROWS = 512
POST_SPLIT = 2
FF_CHUNK = 2048
VMEM_LIMIT = 60 * 1024 * 1024

Q_SCALE = (QK_NOPE + QK_ROPE) ** -0.5 * math.log2(math.e)


def _rms(x, g):
    y = x * lax.rsqrt(jnp.mean(x * x, axis=-1, keepdims=True) + EPS)
    return y * g


def _dot(a, b):
    return jnp.dot(a, b, preferred_element_type=jnp.float32)


def _qkv_kernel(x_ref, pos_ref, invf_ref, g_ref, win_ref, gq_ref, wuq_ref, gkv_ref,
                wuk_ref, wuvt_ref, q_ref, k_ref, vt_ref):
    bf = jnp.bfloat16
    ts = x_ref.shape[0] // QKV_SPLIT
    tp = ts // ROPE_PACK

    def rows_chain(part):
        rows = slice(part * ts, (part + 1) * ts)
        a = _rms(x_ref[rows, :], g_ref[...]).astype(bf)
        yield
        proj = _dot(a, win_ref[...])
        yield
        q_n = _rms(proj[:, :Q_LORA], gq_ref[...]).astype(bf)
        kv_n = _rms(proj[:, Q_LORA:Q_LORA + KV_LORA], gkv_ref[...]).astype(bf)
        k_rope = proj[:, Q_LORA + KV_LORA:]
        yield
        q_all = _dot(q_n, wuq_ref[...])
        yield
        k_all = _dot(kv_n, wuk_ref[...])
        v_t = lax.dot_general(wuvt_ref[...], kv_n, (((1,), (1,)), ((), ())),
                              preferred_element_type=jnp.float32)
        v_row = lax.broadcasted_iota(jnp.int32, v_t.shape, 0) % V_ROWS
        vt_ref[0, :, rows] = jnp.where(v_row == V_HEAD, 1.0, v_t).astype(bf)
        yield

        ang = pos_ref[part * tp:(part + 1) * tp, :].astype(jnp.float32) * invf_ref[...]
        cos_p, sin_p = jnp.cos(ang), jnp.sin(ang)
        lane = lax.broadcasted_iota(jnp.int32, (tp, LANES), 1)
        in_x1 = (lane >= ROPE_LANE0) & (lane < ROPE_LANE0 + ROPE_HALF)
        in_x2 = (lane >= ROPE_LANE0 + ROPE_HALF) & (lane < ROPE_LANE0 + QK_ROPE)
        cos_t, sin_t = [], []
        for qr in range(ROPE_PACK):
            shift = (ROPE_LANE0 - QK_ROPE * qr) % LANES
            c = pltpu.roll(cos_p, shift, 1) if shift else cos_p
            s = pltpu.roll(sin_p, shift, 1) if shift else sin_p
            cos_t.append(jnp.where(lane < ROPE_LANE0, 1.0, jnp.where(in_x1 | in_x2, c, 0.0)))
            sin_t.append(jnp.where(in_x1, -s, jnp.where(in_x2, s, 0.0)))
        cos_t = jnp.concatenate(cos_t, axis=0)
        sin_t = jnp.concatenate(sin_t, axis=0)
        yield

        def rope(z):
            return z * cos_t + pltpu.roll(z, LANES - QK_ROPE, 1) * sin_t

        k_rope = rope(k_rope)
        for h in range(N_HEADS):
            sl = slice(h * HEAD_PAD, (h + 1) * HEAD_PAD)
            q_ref[rows, sl] = (rope(q_all[:, sl]) * Q_SCALE).astype(bf)
            k_ref[rows, sl] = (k_all[:, sl] + k_rope).astype(bf)
            if h % 2:
                yield

    chains = [rows_chain(part) for part in range(QKV_SPLIT)]
    while chains:
        chains = [c for c in chains if next(c, True) is None]


def _qkv_call(x, pos_packed, invf, g, win, gq, wuq, gkv, wuk, wuvt):
    s = x.shape[0]
    tm = V_CHUNK
    row = lambda i: (i, 0)
    fixed = lambda i: (0, 0)
    full = lambda arr: pl.BlockSpec(arr.shape, fixed)
    qk = jax.ShapeDtypeStruct((s, N_HEADS * HEAD_PAD), jnp.bfloat16)
    vt = jax.ShapeDtypeStruct((s // tm, N_HEADS * V_ROWS, tm), jnp.bfloat16)
    return pl.pallas_call(
        _qkv_kernel,
        grid=(s // tm,),
        in_specs=[pl.BlockSpec((tm, D_MODEL), row), pl.BlockSpec((tm // ROPE_PACK, LANES), row),
                  full(invf), full(g), full(win), full(gq), full(wuq), full(gkv), full(wuk),
                  full(wuvt)],
        out_specs=[pl.BlockSpec((tm, N_HEADS * HEAD_PAD), row),
                   pl.BlockSpec((tm, N_HEADS * HEAD_PAD), row),
                   pl.BlockSpec((1, N_HEADS * V_ROWS, tm), lambda i: (i, 0, 0))],
        out_shape=[qk, qk, vt],
        compiler_params=pltpu.CompilerParams(
            dimension_semantics=("arbitrary",), vmem_limit_bytes=VMEM_LIMIT),
        name="qkv_proj",
    )(x, pos_packed, invf, g, win, gq, wuq, gkv, wuk, wuvt)


class _Unit(NamedTuple):
    head: int
    col: int
    kv_start: Any
    width: int
    shift: Optional[int]


def _attn_kernel(q_ref, k_ref, vt_ref, o_ref, m_ref, acc_ref, *s_bufs, tk):
    tq = q_ref.shape[0]
    n_buf = len(s_bufs)
    ahead = n_buf - 1
    i = pl.program_id(1)
    m_ref[...] = jnp.full(m_ref.shape, MASK_VALUE, jnp.float32)
    acc_ref[...] = jnp.zeros(acc_ref.shape, jnp.float32)

    def scores(u, slot):
        sl = slice(u.head * HEAD_PAD, (u.head + 1) * HEAD_PAD)
        s = lax.dot_general(k_ref[pl.ds(u.kv_start, u.width), sl], q_ref[u.col:u.col + ATT_TILE, sl],
                            (((1,), (1,)), ((), ())), preferred_element_type=jnp.float32)
        if u.shift is not None:
            r = lax.broadcasted_iota(jnp.int32, s.shape, 0)
            c = lax.broadcasted_iota(jnp.int32, s.shape, 1)
            s = jnp.where(r <= c + u.shift, s, MASK_VALUE)
        s_bufs[slot][:u.width, :] = s
        rows = u.width // MAX_CHAINS
        parts = [jnp.max(s[g * rows:(g + 1) * rows], axis=0, keepdims=True)
                 for g in range(MAX_CHAINS)]
        while len(parts) > 1:
            parts = [jnp.maximum(a, b) for a, b in zip(parts[::2], parts[1::2])]
        return parts[0]

    def update(u, slot, col_max):
        qc = slice(u.col, u.col + ATT_TILE)
        m_prev = m_ref[u.head, :, qc]
        m_new = jnp.maximum(m_prev, col_max)
        alpha = jnp.exp2(m_prev - m_new)
        p = jnp.exp2(s_bufs[slot][:u.width, :] - m_new).astype(jnp.bfloat16)
        vrows = slice(u.head * V_ROWS, (u.head + 1) * V_ROWS)
        slab0 = u.kv_start // V_CHUNK
        pv = None
        for c, lo in enumerate(range(0, u.width, V_CHUNK)):
            n = min(V_CHUNK, u.width - lo)
            part = _dot(vt_ref[slab0 + c, vrows, :n], p[lo:lo + n])
            pv = part if pv is None else pv + part
        acc_ref[u.head, :, qc] = alpha * acc_ref[u.head, :, qc] + pv
        m_ref[u.head, :, qc] = m_new

    def run(units, ready=(), then=(), after=None):
        seq = list(units) + list(then)
        maxes = dict(enumerate(ready))
        for n in range(len(maxes), min(ahead, len(seq))):
            maxes[n] = scores(seq[n], n % n_buf)
        for n, u in enumerate(units):
            if n + ahead < len(seq):
                maxes[n + ahead] = scores(seq[n + ahead], (n + ahead) % n_buf)
            update(u, n % n_buf, maxes.pop(n))
            if after is not None:
                after(u)
        return tuple(maxes[len(units) + n] for n in range(len(then)))

    tile_order = list(range(tk, tq, ATT_TILE)) + list(range(0, tk, ATT_TILE))

    def full_units(block):
        kv_start = pl.multiple_of(block * tk, tk)
        return [_Unit(hh, c0, kv_start, tk, None) for hh in range(2) for c0 in tile_order]

    n_main = (i * tq) // tk
    assert tq == 2 * tk and ahead <= tk // ATT_TILE and (2 * (tq // ATT_TILE)) % n_buf == 0

    def body(j, first_maxes):
        return run(full_units(j), ready=first_maxes, then=full_units(j + 1)[:ahead])

    first = full_units(0)[:ahead]
    first_maxes = lax.fori_loop(0, n_main, body,
                                tuple(scores(u, n) for n, u in enumerate(first)))

    diag_start = pl.multiple_of(i * tq, tq)
    quarter = [u for u in full_units(n_main) if u.col >= tk]
    masked = []
    for c0 in range(tk - ATT_TILE, -ATT_TILE, -ATT_TILE):
        for hh in range(2):
            masked.append(_Unit(hh, c0, diag_start, c0 + ATT_TILE, c0))
        for hh in range(2):
            masked.append(_Unit(hh, c0 + tk, diag_start + tk, c0 + ATT_TILE, c0))

    def finish(u):
        if u.head != 1 or u.shift is None:
            return
        qc = slice(u.col, u.col + ATT_TILE)
        o_t = jnp.concatenate(
            [acc_ref[hh, :V_HEAD, qc] / acc_ref[hh, V_HEAD:V_HEAD + 1, qc] for hh in range(2)],
            axis=0)
        o_ref[qc, :] = o_t.T.astype(o_ref.dtype)

    run(quarter + masked, ready=first_maxes, after=finish)


def _attn_call(q, k, vt):
    s = q.shape[0]
    tq = ATT_Q
    pair = 2 * HEAD_PAD
    return pl.pallas_call(
        functools.partial(_attn_kernel, tk=ATT_K),
        grid=(N_HEADS // 2, s // tq),
        in_specs=[pl.BlockSpec((tq, pair), lambda hp, i: (i, hp)),
                  pl.BlockSpec((s, pair), lambda hp, i: (0, hp)),
                  pl.BlockSpec((s // V_CHUNK, 2 * V_ROWS, V_CHUNK), lambda hp, i: (0, hp, 0))],
        out_specs=pl.BlockSpec((tq, 2 * V_HEAD), lambda hp, i: (i, hp)),
        out_shape=jax.ShapeDtypeStruct((s, N_HEADS * V_HEAD), jnp.bfloat16),
        scratch_shapes=[pltpu.VMEM((2, 1, tq), jnp.float32),
                        pltpu.VMEM((2, V_ROWS, tq), jnp.float32),
                        ] + [pltpu.VMEM((ATT_K, ATT_TILE), jnp.float32)] * S_BUFS,
        compiler_params=pltpu.CompilerParams(
            dimension_semantics=("arbitrary", "arbitrary"), vmem_limit_bytes=VMEM_LIMIT),
        name="mla_attention",
    )(q, k, vt)


def _post_kernel(x_ref, halo_ref, attn_ref, p_ref, g_pre_ref, wpool_in_ref, wgate_ref, bgate_ref,
                 wpool_ref, pscale_ref, wba_ref, wbp_ref, wout_ref, g_post_ref, g_mlp_ref,
                 wff1_ref, wff2_ref, g_pmlp_ref, wpe_ref, wpg_ref, g_ple_ref, o_ref, ext_ref):
    bf = jnp.bfloat16
    tm = x_ref.shape[0]
    ts = tm // POST_SPLIT
    i = pl.program_id(0)

    def rows_chain(part):
        r0 = part * ts
        rows = slice(r0, r0 + ts)
        x = x_ref[rows, :]
        a = _rms(x, g_pre_ref[...]).astype(bf)
        yield

        if part == 0:
            x_halo, have_halo = halo_ref[...], i > 0
        else:
            x_halo, have_halo = x_ref[r0 - POOL_HALO:r0, :], True
        u_halo = _dot(_rms(x_halo, g_pre_ref[...]).astype(bf), wpool_in_ref[...])
        u = _dot(a, wpool_in_ref[...])
        ext = ext_ref.at[part]
        ext[0:POOL_HALO, :] = jnp.where(have_halo, u_halo, 0.0)
        ext[POOL_HALO:, :] = u
        yield
        gates = jax.nn.sigmoid(_dot(a, wgate_ref[...]) + bgate_ref[...])
        attn_branch = gates[:, :D_MODEL] * _dot(attn_ref[rows, :], wba_ref[...])
        yield
        t = i * tm + r0 + lax.broadcasted_iota(jnp.int32, (ts, 1), 0)
        pooled = []
        for g, w in enumerate(POOL_WINDOWS):
            cols = slice(g * POOL_GROUP, (g + 1) * POOL_GROUP)
            wsum = ext[:, cols]
            shift = 1
            while shift < w:
                wsum = wsum + pltpu.roll(wsum, shift, 0)
                shift *= 2
            wsum = wsum[POOL_HALO:, :]
            cnt = jnp.minimum(t + 1, w).astype(jnp.float32)
            d = wsum / cnt - u[:, cols]
            pooled.append(_dot(d.astype(bf), wpool_ref[g]))
        pooled = jnp.concatenate(pooled, axis=1) * pscale_ref[...]
        yield

        merged = attn_branch + gates[:, D_MODEL:] * _dot(pooled.astype(bf), wbp_ref[...])
        yield
        y = _dot(merged.astype(bf), wout_ref[...])
        yield
        h = x + _rms(y, g_post_ref[...])

        m = _rms(h, g_mlp_ref[...]).astype(bf)
        yield
        f = jnp.zeros((ts, D_MODEL), jnp.float32)
        for c in range(D_FF // FF_CHUNK):
            cols = slice(c * FF_CHUNK, (c + 1) * FF_CHUNK)
            hid = jnp.square(jnp.maximum(_dot(m, wff1_ref[:, cols]), 0.0))
            f = f + _dot(hid.astype(bf), wff2_ref[cols, :])
            yield
        h = h + _rms(f, g_pmlp_ref[...])
        yield

        e = _dot(p_ref[rows, :].astype(bf), wpe_ref[...])
        pg = jax.nn.sigmoid(_dot(h.astype(bf), wpg_ref[...]))
        yield
        o_ref[rows, :] = h + _rms(pg * e, g_ple_ref[...])

    chains = [rows_chain(part) for part in range(POST_SPLIT)]
    while chains:
        chains = [c for c in chains if next(c, True) is None]


def _post_call(x, attn, p, g_pre, wpool_in, wgate, bgate, wpool, pscale, wba, wbp, wout,
               g_post, g_mlp, wff1, wff2, g_pmlp, wpe, wpg, g_ple):
    s = x.shape[0]
    tm = POST_ROWS
    row = lambda i: (i, 0)
    halo_blocks = tm // POOL_HALO

    def const(arr):
        zeros = (0,) * arr.ndim
        return pl.BlockSpec(arr.shape, lambda i: zeros, pipeline_mode=pl.Buffered(1))

    weights = (g_pre, wpool_in, wgate, bgate, wpool, pscale, wba, wbp, wout, g_post, g_mlp,
               wff1, wff2, g_pmlp, wpe, wpg, g_ple)
    return pl.pallas_call(
        _post_kernel,
        grid=(s // tm,),
        in_specs=[pl.BlockSpec((tm, D_MODEL), row),
                  pl.BlockSpec((POOL_HALO, D_MODEL),
                               lambda i: (jnp.maximum(i * halo_blocks - 1, 0), 0)),
                  pl.BlockSpec((tm, N_HEADS * V_HEAD), row),
                  pl.BlockSpec((tm, PLE_DIM), row)] + [const(w) for w in weights],
        out_specs=pl.BlockSpec((tm, D_MODEL), row),
        out_shape=jax.ShapeDtypeStruct((s, D_MODEL), jnp.float32),
        scratch_shapes=[pltpu.VMEM((POST_SPLIT, tm // POST_SPLIT + POOL_HALO, POOL_WIDTH),
                                   jnp.float32)],
        compiler_params=pltpu.CompilerParams(
            dimension_semantics=("arbitrary",), vmem_limit_bytes=VMEM_LIMIT),
        name="post_attention",
    )(x, x, attn, p, *weights)


def _pad_heads(w, per_head):
    kdim = w.shape[0]
    w = w.reshape(kdim, N_HEADS, per_head)
    w = jnp.pad(w, ((0, 0), (0, 0), (0, HEAD_PAD - per_head)))
    return w.reshape(kdim, N_HEADS * HEAD_PAD)


def _rope_tile(nope, rope):
    x1, x2 = rope[..., :ROPE_HALF], rope[..., ROPE_HALF:]
    return jnp.concatenate([nope, x1, x2, x2, x1], axis=-1)


def _layer(h, p, pos, invf, g_pre_mix, w_in, b_gate, g_q, w_uq, g_kv, w_ukv, w_pool, pool_scale,
           w_branch_attn, w_branch_pool, w_out, g_post_mix, g_pre_mlp, w_ff1, w_ff2, g_post_mlp,
           w_ple_proj, w_ple_gate, g_ple):
    bf = jnp.bfloat16
    row = lambda v: v.reshape(1, -1)
    o_kv = Q_LORA
    o_kr = o_kv + KV_LORA
    o_pool = o_kr + QK_ROPE
    o_gate = o_pool + POOL_WIDTH

    w_kr = _rope_tile(jnp.zeros((D_MODEL, QK_NOPE), w_in.dtype), w_in[:, o_kr:o_pool])
    w_in_qkv = jnp.concatenate([w_in[:, :o_kr], w_kr], axis=1).astype(bf)
    w_uq_h = w_uq.reshape(Q_LORA, N_HEADS, QK_NOPE + QK_ROPE)
    w_uq_p = _rope_tile(w_uq_h[..., :QK_NOPE], w_uq_h[..., QK_NOPE:]).reshape(Q_LORA, -1).astype(bf)
    w_ukv_h = w_ukv.reshape(KV_LORA, N_HEADS, QK_NOPE + V_HEAD)
    w_uk_p = _pad_heads(w_ukv_h[:, :, :QK_NOPE].reshape(KV_LORA, -1), QK_NOPE).astype(bf)
    w_uv_t = jnp.pad(jnp.transpose(w_ukv_h[:, :, QK_NOPE:], (1, 2, 0)),
                     ((0, 0), (0, V_ROWS - V_HEAD), (0, 0))).reshape(N_HEADS * V_ROWS, KV_LORA).astype(bf)

    q, k, vt = _qkv_call(h, pos, invf, row(g_pre_mix), w_in_qkv, row(g_q), w_uq_p, row(g_kv),
                         w_uk_p, w_uv_t)
    attn = _attn_call(q, k, vt)
    return _post_call(
        h, attn, p, row(g_pre_mix), w_in[:, o_pool:o_gate].astype(bf), w_in[:, o_gate:].astype(bf),
        row(b_gate), w_pool.astype(bf), row(pool_scale), w_branch_attn.astype(bf),
        w_branch_pool.astype(bf), w_out.astype(bf), row(g_post_mix), row(g_pre_mlp),
        w_ff1.astype(bf), w_ff2.astype(bf), row(g_post_mlp), w_ple_proj.astype(bf),
        w_ple_gate.astype(bf), row(g_ple))


@jax.jit
def kernel(x, p, positions, g_pre_mix, w_in, b_gate, g_q, w_uq, g_kv, w_ukv, w_pool, pool_scale,
           w_branch_attn, w_branch_pool, w_out, g_post_mix, g_pre_mlp, w_ff1, w_ff2, g_post_mlp,
           w_ple_proj, w_ple_gate, g_ple):
    batch, seq, _ = x.shape
    depth = w_in.shape[0]
    inv_freq = ROPE_THETA ** (-jnp.arange(0, QK_ROPE, 2, dtype=jnp.float32) / QK_ROPE)
    invf = jnp.tile(inv_freq, 2 * ROPE_PACK).reshape(1, LANES)
    ts = V_CHUNK // QKV_SPLIT
    outs = []
    for b in range(batch):
        h = x[b]
        pos = positions[b].reshape(seq // ts, ROPE_PACK, ts // ROPE_PACK).transpose(0, 2, 1)
        pos = jnp.repeat(pos.reshape(seq // ROPE_PACK, ROPE_PACK), QK_ROPE, axis=1)
        for l in range(depth):
            h = _layer(h, p[l, b], pos, invf, g_pre_mix[l], w_in[l], b_gate[l], g_q[l], w_uq[l],
                       g_kv[l], w_ukv[l], w_pool[l], pool_scale[l], w_branch_attn[l],
                       w_branch_pool[l], w_out[l], g_post_mix[l], g_pre_mlp[l], w_ff1[l], w_ff2[l],
                       g_post_mlp[l], w_ple_proj[l], w_ple_gate[l], g_ple[l])
        outs.append(h)
    return jnp.stack(outs, axis=0)
```

```python
import functools
import math
from typing import Any, NamedTuple, Optional

import jax
import jax.numpy as jnp
from jax import lax
from jax.experimental import pallas as pl
from jax.experimental.pallas import tpu as pltpu

D_MODEL = 1024
PLE_DIM = 256
N_HEADS = 8
QK_NOPE = 64
QK_ROPE = 32
V_HEAD = 64
Q_LORA = 384
KV_LORA = 256
POOL_WINDOWS = (2, 4, 8, 16)
POOL_GROUP = 128
POOL_WIDTH = POOL_GROUP * len(POOL_WINDOWS)
D_FF = 4 * D_MODEL
ROPE_THETA = 10000.0
EPS = 1e-6

LANES = 128
HEAD_PAD = LANES
ROPE_LANE0 = QK_NOPE
ROPE_HALF = QK_ROPE // 2
POOL_HALO = max(POOL_WINDOWS)
MASK_VALUE = float("-inf")

V_ROWS = 80
V_CHUNK = 512
QKV_SPLIT = 2
ROPE_PACK = LANES // QK_ROPE
ATT_Q = 4096
ATT_K = 2048
ATT_TILE = 256
MAX_CHAINS = 2
S_BUFS = 4
POST_ROWS = 512
POST_SPLIT = 2
FF_CHUNK = 4096
VMEM_LIMIT = 60 * 1024 * 1024

Q_SCALE = (QK_NOPE + QK_ROPE) ** -0.5 * math.log2(math.e)


def _rms(x, g):
    y = x * lax.rsqrt(jnp.mean(x * x, axis=-1, keepdims=True) + EPS)
    return y * g


def _dot(a, b):
    return jnp.dot(a, b, preferred_element_type=jnp.float32)


def _qkv_kernel(x_ref, pos_ref, invf_ref, g_ref, win_ref, gq_ref, wuq_ref, gkv_ref,
                wuk_ref, wuvt_ref, q_ref, k_ref, vt_ref):
    bf = jnp.bfloat16
    ts = x_ref.shape[0] // QKV_SPLIT
    tp = ts // ROPE_PACK

    def rows_chain(part):
        rows = slice(part * ts, (part + 1) * ts)
        a = _rms(x_ref[rows, :], g_ref[...]).astype(bf)
        yield
        proj = _dot(a, win_ref[...])
        yield
        q_n = _rms(proj[:, :Q_LORA], gq_ref[...]).astype(bf)
        kv_n = _rms(proj[:, Q_LORA:Q_LORA + KV_LORA], gkv_ref[...]).astype(bf)
        k_rope = proj[:, Q_LORA + KV_LORA:]
        yield
        q_all = _dot(q_n, wuq_ref[...])
        yield
        k_all = _dot(kv_n, wuk_ref[...])
        v_t = lax.dot_general(wuvt_ref[...], kv_n, (((1,), (1,)), ((), ())),
                              preferred_element_type=jnp.float32)
        v_row = lax.broadcasted_iota(jnp.int32, v_t.shape, 0) % V_ROWS
        vt_ref[0, :, rows] = jnp.where(v_row == V_HEAD, 1.0, v_t).astype(bf)
        yield

        ang = pos_ref[part * tp:(part + 1) * tp, :].astype(jnp.float32) * invf_ref[...]
        cos_p, sin_p = jnp.cos(ang), jnp.sin(ang)
        lane = lax.broadcasted_iota(jnp.int32, (tp, LANES), 1)
        in_x1 = (lane >= ROPE_LANE0) & (lane < ROPE_LANE0 + ROPE_HALF)
        in_x2 = (lane >= ROPE_LANE0 + ROPE_HALF) & (lane < ROPE_LANE0 + QK_ROPE)
        cos_t, sin_t = [], []
        for qr in range(ROPE_PACK):
            shift = (ROPE_LANE0 - QK_ROPE * qr) % LANES
            c = pltpu.roll(cos_p, shift, 1) if shift else cos_p
            s = pltpu.roll(sin_p, shift, 1) if shift else sin_p
            cos_t.append(jnp.where(lane < ROPE_LANE0, 1.0, jnp.where(in_x1 | in_x2, c, 0.0)))
            sin_t.append(jnp.where(in_x1, -s, jnp.where(in_x2, s, 0.0)))
        cos_t = jnp.concatenate(cos_t, axis=0)
        sin_t = jnp.concatenate(sin_t, axis=0)
        yield

        def rope(z):
            return z * cos_t + pltpu.roll(z, LANES - QK_ROPE, 1) * sin_t

        k_rope = rope(k_rope)
        for h in range(N_HEADS):
            sl = slice(h * HEAD_PAD, (h + 1) * HEAD_PAD)
            q_ref[rows, sl] = (rope(q_all[:, sl]) * Q_SCALE).astype(bf)
            k_ref[rows, sl] = (k_all[:, sl] + k_rope).astype(bf)
            if h % 2:
                yield

    chains = [rows_chain(part) for part in range(QKV_SPLIT)]
    while chains:
        chains = [c for c in chains if next(c, True) is None]


def _qkv_call(x, pos_packed, invf, g, win, gq, wuq, gkv, wuk, wuvt):
    s = x.shape[0]
    tm = V_CHUNK
    row = lambda i: (i, 0)
    fixed = lambda i: (0, 0)
    full = lambda arr: pl.BlockSpec(arr.shape, fixed)
    qk = jax.ShapeDtypeStruct((s, N_HEADS * HEAD_PAD), jnp.bfloat16)
    vt = jax.ShapeDtypeStruct((s // tm, N_HEADS * V_ROWS, tm), jnp.bfloat16)
    return pl.pallas_call(
        _qkv_kernel,
        grid=(s // tm,),
        in_specs=[pl.BlockSpec((tm, D_MODEL), row), pl.BlockSpec((tm // ROPE_PACK, LANES), row),
                  full(invf), full(g), full(win), full(gq), full(wuq), full(gkv), full(wuk),
                  full(wuvt)],
        out_specs=[pl.BlockSpec((tm, N_HEADS * HEAD_PAD), row),
                   pl.BlockSpec((tm, N_HEADS * HEAD_PAD), row),
                   pl.BlockSpec((1, N_HEADS * V_ROWS, tm), lambda i: (i, 0, 0))],
        out_shape=[qk, qk, vt],
        compiler_params=pltpu.CompilerParams(
            dimension_semantics=("arbitrary",), vmem_limit_bytes=VMEM_LIMIT),
        name="qkv_proj",
    )(x, pos_packed, invf, g, win, gq, wuq, gkv, wuk, wuvt)


class _Unit(NamedTuple):
    head: int
    col: int
    kv_start: Any
    width: int
    shift: Optional[int]


def _attn_kernel(q_ref, k_ref, vt_ref, o_ref, m_ref, acc_ref, *s_bufs, tk):
    tq = q_ref.shape[0]
    n_buf = len(s_bufs)
    ahead = n_buf - 1
    i = pl.program_id(1)
    m_ref[...] = jnp.full(m_ref.shape, MASK_VALUE, jnp.float32)
    acc_ref[...] = jnp.zeros(acc_ref.shape, jnp.float32)

    def scores(u, slot):
        sl = slice(u.head * HEAD_PAD, (u.head + 1) * HEAD_PAD)
        s = lax.dot_general(k_ref[pl.ds(u.kv_start, u.width), sl], q_ref[u.col:u.col + ATT_TILE, sl],
                            (((1,), (1,)), ((), ())), preferred_element_type=jnp.float32)
        if u.shift is not None:
            r = lax.broadcasted_iota(jnp.int32, s.shape, 0)
            c = lax.broadcasted_iota(jnp.int32, s.shape, 1)
            s = jnp.where(r <= c + u.shift, s, MASK_VALUE)
        s_bufs[slot][:u.width, :] = s
        rows = u.width // MAX_CHAINS
        parts = [jnp.max(s[g * rows:(g + 1) * rows], axis=0, keepdims=True)
                 for g in range(MAX_CHAINS)]
        while len(parts) > 1:
            parts = [jnp.maximum(a, b) for a, b in zip(parts[::2], parts[1::2])]
        return parts[0]

    def update(u, slot, col_max):
        qc = slice(u.col, u.col + ATT_TILE)
        m_prev = m_ref[u.head, :, qc]
        m_new = jnp.maximum(m_prev, col_max)
        alpha = jnp.exp2(m_prev - m_new)
        p = jnp.exp2(s_bufs[slot][:u.width, :] - m_new).astype(jnp.bfloat16)
        vrows = slice(u.head * V_ROWS, (u.head + 1) * V_ROWS)
        slab0 = u.kv_start // V_CHUNK
        pv = None
        for c, lo in enumerate(range(0, u.width, V_CHUNK)):
            n = min(V_CHUNK, u.width - lo)
            part = _dot(vt_ref[slab0 + c, vrows, :n], p[lo:lo + n])
            pv = part if pv is None else pv + part
        acc_ref[u.head, :, qc] = alpha * acc_ref[u.head, :, qc] + pv
        m_ref[u.head, :, qc] = m_new

    def run(units, ready=(), then=(), after=None):
        seq = list(units) + list(then)
        maxes = dict(enumerate(ready))
        for n in range(len(maxes), min(ahead, len(seq))):
            maxes[n] = scores(seq[n], n % n_buf)
        for n, u in enumerate(units):
            if n + ahead < len(seq):
                maxes[n + ahead] = scores(seq[n + ahead], (n + ahead) % n_buf)
            update(u, n % n_buf, maxes.pop(n))
            if after is not None:
                after(u)
        return tuple(maxes[len(units) + n] for n in range(len(then)))

    tile_order = list(range(tk, tq, ATT_TILE)) + list(range(0, tk, ATT_TILE))

    def full_units(block):
        kv_start = pl.multiple_of(block * tk, tk)
        return [_Unit(hh, c0, kv_start, tk, None) for hh in range(2) for c0 in tile_order]

    n_main = (i * tq) // tk
    assert tq == 2 * tk and ahead <= tk // ATT_TILE and (2 * (tq // ATT_TILE)) % n_buf == 0

    def body(j, first_maxes):
        return run(full_units(j), ready=first_maxes, then=full_units(j + 1)[:ahead])

    first = full_units(0)[:ahead]
    first_maxes = lax.fori_loop(0, n_main, body,
                                tuple(scores(u, n) for n, u in enumerate(first)))

    diag_start = pl.multiple_of(i * tq, tq)
    quarter = [u for u in full_units(n_main) if u.col >= tk]
    masked = []
    for c0 in range(tk - ATT_TILE, -ATT_TILE, -ATT_TILE):
        for hh in range(2):
            masked.append(_Unit(hh, c0, diag_start, c0 + ATT_TILE, c0))
        for hh in range(2):
            masked.append(_Unit(hh, c0 + tk, diag_start + tk, c0 + ATT_TILE, c0))

    def finish(u):
        if u.head != 1 or u.shift is None:
            return
        qc = slice(u.col, u.col + ATT_TILE)
        o_t = jnp.concatenate(
            [acc_ref[hh, :V_HEAD, qc] / acc_ref[hh, V_HEAD:V_HEAD + 1, qc] for hh in range(2)],
            axis=0)
        o_ref[qc, :] = o_t.T.astype(o_ref.dtype)

    run(quarter + masked, ready=first_maxes, after=finish)


def _attn_call(q, k, vt):
    s = q.shape[0]
    tq = ATT_Q
    pair = 2 * HEAD_PAD
    return pl.pallas_call(
        functools.partial(_attn_kernel, tk=ATT_K),
        grid=(N_HEADS // 2, s // tq),
        in_specs=[pl.BlockSpec((tq, pair), lambda hp, i: (i, hp)),
                  pl.BlockSpec((s, pair), lambda hp, i: (0, hp)),
                  pl.BlockSpec((s // V_CHUNK, 2 * V_ROWS, V_CHUNK), lambda hp, i: (0, hp, 0))],
        out_specs=pl.BlockSpec((tq, 2 * V_HEAD), lambda hp, i: (i, hp)),
        out_shape=jax.ShapeDtypeStruct((s, N_HEADS * V_HEAD), jnp.bfloat16),
        scratch_shapes=[pltpu.VMEM((2, 1, tq), jnp.float32),
                        pltpu.VMEM((2, V_ROWS, tq), jnp.float32),
                        ] + [pltpu.VMEM((ATT_K, ATT_TILE), jnp.float32)] * S_BUFS,
        compiler_params=pltpu.CompilerParams(
            dimension_semantics=("arbitrary", "arbitrary"), vmem_limit_bytes=VMEM_LIMIT),
        name="mla_attention",
    )(q, k, vt)


def _post_kernel(x_ref, halo_ref, attn_ref, p_ref, g_pre_ref, wpool_in_ref, wgate_ref, bgate_ref,
                 wpool_ref, pscale_ref, wba_ref, wbp_ref, wout_ref, g_post_ref, g_mlp_ref,
                 wff1_ref, wff2_ref, g_pmlp_ref, wpe_ref, wpg_ref, g_ple_ref, o_ref, ext_ref):
    bf = jnp.bfloat16
    tm = x_ref.shape[0]
    ts = tm // POST_SPLIT
    i = pl.program_id(0)

    def rows_chain(part):
        r0 = part * ts
        rows = slice(r0, r0 + ts)
        x = x_ref[rows, :]
        a = _rms(x, g_pre_ref[...]).astype(bf)
        yield

        if part == 0:
            x_halo, have_halo = halo_ref[...], i > 0
        else:
            x_halo, have_halo = x_ref[r0 - POOL_HALO:r0, :], True
        u_halo = _dot(_rms(x_halo, g_pre_ref[...]).astype(bf), wpool_in_ref[...])
        u = _dot(a, wpool_in_ref[...])
        ext = ext_ref.at[part]
        ext[0:POOL_HALO, :] = jnp.where(have_halo, u_halo, 0.0)
        ext[POOL_HALO:, :] = u
        yield
        gates = jax.nn.sigmoid(_dot(a, wgate_ref[...]) + bgate_ref[...])
        attn_branch = gates[:, :D_MODEL] * _dot(attn_ref[rows, :], wba_ref[...])
        yield
        t = i * tm + r0 + lax.broadcasted_iota(jnp.int32, (ts, 1), 0)
        pooled = []
        for g, w in enumerate(POOL_WINDOWS):
            cols = slice(g * POOL_GROUP, (g + 1) * POOL_GROUP)
            wsum = ext[:, cols]
            shift = 1
            while shift < w:
                wsum = wsum + pltpu.roll(wsum, shift, 0)
                shift *= 2
            wsum = wsum[POOL_HALO:, :]
            cnt = jnp.minimum(t + 1, w).astype(jnp.float32)
            d = wsum / cnt - u[:, cols]
            pooled.append(_dot(d.astype(bf), wpool_ref[g]))
        pooled = jnp.concatenate(pooled, axis=1) * pscale_ref[...]
        yield

        merged = attn_branch + gates[:, D_MODEL:] * _dot(pooled.astype(bf), wbp_ref[...])
        yield
        y = _dot(merged.astype(bf), wout_ref[...])
        yield
        h = x + _rms(y, g_post_ref[...])

        m = _rms(h, g_mlp_ref[...]).astype(bf)
        yield
        f = jnp.zeros((ts, D_MODEL), jnp.float32)
        for c in range(D_FF // FF_CHUNK):
            cols = slice(c * FF_CHUNK, (c + 1) * FF_CHUNK)
            hid = jnp.square(jnp.maximum(_dot(m, wff1_ref[:, cols]), 0.0))
            f = f + _dot(hid.astype(bf), wff2_ref[cols, :])
            yield
        h = h + _rms(f, g_pmlp_ref[...])
        yield

        e = _dot(p_ref[rows, :].astype(bf), wpe_ref[...])
        pg = jax.nn.sigmoid(_dot(h.astype(bf), wpg_ref[...]))
        yield
        o_ref[rows, :] = h + _rms(pg * e, g_ple_ref[...])

    chains = [rows_chain(part) for part in range(POST_SPLIT)]
    while chains:
        chains = [c for c in chains if next(c, True) is None]


def _post_call(x, attn, p, g_pre, wpool_in, wgate, bgate, wpool, pscale, wba, wbp, wout,
               g_post, g_mlp, wff1, wff2, g_pmlp, wpe, wpg, g_ple):
    s = x.shape[0]
    tm = POST_ROWS
    row = lambda i: (i, 0)
    halo_blocks = tm // POOL_HALO

    def const(arr):
        zeros = (0,) * arr.ndim
        return pl.BlockSpec(arr.shape, lambda i: zeros, pipeline_mode=pl.Buffered(1))

    weights = (g_pre, wpool_in, wgate, bgate, wpool, pscale, wba, wbp, wout, g_post, g_mlp,
               wff1, wff2, g_pmlp, wpe, wpg, g_ple)
    return pl.pallas_call(
        _post_kernel,
        grid=(s // tm,),
        in_specs=[pl.BlockSpec((tm, D_MODEL), row),
                  pl.BlockSpec((POOL_HALO, D_MODEL),
                               lambda i: (jnp.maximum(i * halo_blocks - 1, 0), 0)),
                  pl.BlockSpec((tm, N_HEADS * V_HEAD), row),
                  pl.BlockSpec((tm, PLE_DIM), row)] + [const(w) for w in weights],
        out_specs=pl.BlockSpec((tm, D_MODEL), row),
        out_shape=jax.ShapeDtypeStruct((s, D_MODEL), jnp.float32),
        scratch_shapes=[pltpu.VMEM((POST_SPLIT, tm // POST_SPLIT + POOL_HALO, POOL_WIDTH),
                                   jnp.float32)],
        compiler_params=pltpu.CompilerParams(
            dimension_semantics=("arbitrary",), vmem_limit_bytes=VMEM_LIMIT),
        name="post_attention",
    )(x, x, attn, p, *weights)


def _pad_heads(w, per_head):
    kdim = w.shape[0]
    w = w.reshape(kdim, N_HEADS, per_head)
    w = jnp.pad(w, ((0, 0), (0, 0), (0, HEAD_PAD - per_head)))
    return w.reshape(kdim, N_HEADS * HEAD_PAD)


def _rope_tile(nope, rope):
    x1, x2 = rope[..., :ROPE_HALF], rope[..., ROPE_HALF:]
    return jnp.concatenate([nope, x1, x2, x2, x1], axis=-1)


def _layer(h, p, pos, invf, g_pre_mix, w_in, b_gate, g_q, w_uq, g_kv, w_ukv, w_pool, pool_scale,
           w_branch_attn, w_branch_pool, w_out, g_post_mix, g_pre_mlp, w_ff1, w_ff2, g_post_mlp,
           w_ple_proj, w_ple_gate, g_ple):
    bf = jnp.bfloat16
    row = lambda v: v.reshape(1, -1)
    o_kv = Q_LORA
    o_kr = o_kv + KV_LORA
    o_pool = o_kr + QK_ROPE
    o_gate = o_pool + POOL_WIDTH

    w_kr = _rope_tile(jnp.zeros((D_MODEL, QK_NOPE), w_in.dtype), w_in[:, o_kr:o_pool])
    w_in_qkv = jnp.concatenate([w_in[:, :o_kr], w_kr], axis=1).astype(bf)
    w_uq_h = w_uq.reshape(Q_LORA, N_HEADS, QK_NOPE + QK_ROPE)
    w_uq_p = _rope_tile(w_uq_h[..., :QK_NOPE], w_uq_h[..., QK_NOPE:]).reshape(Q_LORA, -1).astype(bf)
    w_ukv_h = w_ukv.reshape(KV_LORA, N_HEADS, QK_NOPE + V_HEAD)
    w_uk_p = _pad_heads(w_ukv_h[:, :, :QK_NOPE].reshape(KV_LORA, -1), QK_NOPE).astype(bf)
    w_uv_t = jnp.pad(jnp.transpose(w_ukv_h[:, :, QK_NOPE:], (1, 2, 0)),
                     ((0, 0), (0, V_ROWS - V_HEAD), (0, 0))).reshape(N_HEADS * V_ROWS, KV_LORA).astype(bf)

    q, k, vt = _qkv_call(h, pos, invf, row(g_pre_mix), w_in_qkv, row(g_q), w_uq_p, row(g_kv),
                         w_uk_p, w_uv_t)
    attn = _attn_call(q, k, vt)
    return _post_call(
        h, attn, p, row(g_pre_mix), w_in[:, o_pool:o_gate].astype(bf), w_in[:, o_gate:].astype(bf),
        row(b_gate), w_pool.astype(bf), row(pool_scale), w_branch_attn.astype(bf),
        w_branch_pool.astype(bf), w_out.astype(bf), row(g_post_mix), row(g_pre_mlp),
        w_ff1.astype(bf), w_ff2.astype(bf), row(g_post_mlp), w_ple_proj.astype(bf),
        w_ple_gate.astype(bf), row(g_ple))


@jax.jit
def kernel(x, p, positions, g_pre_mix, w_in, b_gate, g_q, w_uq, g_kv, w_ukv, w_pool, pool_scale,
           w_branch_attn, w_branch_pool, w_out, g_post_mix, g_pre_mlp, w_ff1, w_ff2, g_post_mlp,
           w_ple_proj, w_ple_gate, g_ple):
    batch, seq, _ = x.shape
    depth = w_in.shape[0]
    inv_freq = ROPE_THETA ** (-jnp.arange(0, QK_ROPE, 2, dtype=jnp.float32) / QK_ROPE)
    invf = jnp.tile(inv_freq, 2 * ROPE_PACK).reshape(1, LANES)
    ts = V_CHUNK // QKV_SPLIT
    outs = []
    for b in range(batch):
        h = x[b]
        pos = positions[b].reshape(seq // ts, ROPE_PACK, ts // ROPE_PACK).transpose(0, 2, 1)
        pos = jnp.repeat(pos.reshape(seq // ROPE_PACK, ROPE_PACK), QK_ROPE, axis=1)
        for l in range(depth):
            h = _layer(h, p[l, b], pos, invf, g_pre_mix[l], w_in[l], b_gate[l], g_q[l], w_uq[l],
                       g_kv[l], w_ukv[l], w_pool[l], pool_scale[l], w_branch_attn[l],
                       w_branch_pool[l], w_out[l], g_post_mix[l], g_pre_mlp[l], w_ff1[l], w_ff2[l],
                       g_post_mlp[l], w_ple_proj[l], w_ple_gate[l], g_ple[l])
        outs.append(h)
    return jnp.stack(outs, axis=0)
```

```python
import functools
import math
from typing import Any, NamedTuple, Optional

import jax
import jax.numpy as jnp
from jax import lax
from jax.experimental import pallas as pl
from jax.experimental.pallas import tpu as pltpu

D_MODEL = 1024
PLE_DIM = 256
N_HEADS = 8
QK_NOPE = 64
QK_ROPE = 32
V_HEAD = 64
Q_LORA = 384
KV_LORA = 256
POOL_WINDOWS = (2, 4, 8, 16)
POOL_GROUP = 128
POOL_WIDTH = POOL_GROUP * len(POOL_WINDOWS)
D_FF = 4 * D_MODEL
ROPE_THETA = 10000.0
EPS = 1e-6

LANES = 128
HEAD_PAD = LANES
ROPE_LANE0 = QK_NOPE
ROPE_HALF = QK_ROPE // 2
POOL_HALO = max(POOL_WINDOWS)
MASK_VALUE = float("-inf")

V_ROWS = 80
V_CHUNK = 512
QKV_SPLIT = 2
ROPE_PACK = LANES // QK_ROPE
ATT_Q = 4096
ATT_K = 2048
ATT_TILE = 256
MAX_CHAINS = 2
S_BUFS = 4
POST_ROWS = 512
POST_SPLIT = 2
FF_CHUNK = 2048
VMEM_LIMIT = 60 * 1024 * 1024

Q_SCALE = (QK_NOPE + QK_ROPE) ** -0.5 * math.log2(math.e)


def _rms(x, g):
    y = x * lax.rsqrt(jnp.mean(x * x, axis=-1, keepdims=True) + EPS)
    return y * g


def _dot(a, b):
    return jnp.dot(a, b, preferred_element_type=jnp.float32)


def _qkv_kernel(x_ref, pos_ref, invf_ref, g_ref, win_ref, gq_ref, wuq_ref, gkv_ref,
                wuk_ref, wuvt_ref, q_ref, k_ref, vt_ref):
    bf = jnp.bfloat16
    ts = x_ref.shape[0] // QKV_SPLIT
    tp = ts // ROPE_PACK

    def rows_chain(part):
        rows = slice(part * ts, (part + 1) * ts)
        a = _rms(x_ref[rows, :], g_ref[...]).astype(bf)
        yield
        proj = _dot(a, win_ref[...])
        yield
        q_n = _rms(proj[:, :Q_LORA], gq_ref[...]).astype(bf)
        kv_n = _rms(proj[:, Q_LORA:Q_LORA + KV_LORA], gkv_ref[...]).astype(bf)
        k_rope = proj[:, Q_LORA + KV_LORA:]
        yield
        q_all = _dot(q_n, wuq_ref[...])
        yield
        k_all = _dot(kv_n, wuk_ref[...])
        v_t = lax.dot_general(wuvt_ref[...], kv_n, (((1,), (1,)), ((), ())),
                              preferred_element_type=jnp.float32)
        v_row = lax.broadcasted_iota(jnp.int32, v_t.shape, 0) % V_ROWS
        vt_ref[0, :, rows] = jnp.where(v_row == V_HEAD, 1.0, v_t).astype(bf)
        yield

        ang = pos_ref[part * tp:(part + 1) * tp, :].astype(jnp.float32) * invf_ref[...]
        cos_p, sin_p = jnp.cos(ang), jnp.sin(ang)
        lane = lax.broadcasted_iota(jnp.int32, (tp, LANES), 1)
        in_x1 = (lane >= ROPE_LANE0) & (lane < ROPE_LANE0 + ROPE_HALF)
        in_x2 = (lane >= ROPE_LANE0 + ROPE_HALF) & (lane < ROPE_LANE0 + QK_ROPE)
        cos_t, sin_t = [], []
        for qr in range(ROPE_PACK):
            shift = (ROPE_LANE0 - QK_ROPE * qr) % LANES
            c = pltpu.roll(cos_p, shift, 1) if shift else cos_p
            s = pltpu.roll(sin_p, shift, 1) if shift else sin_p
            cos_t.append(jnp.where(lane < ROPE_LANE0, 1.0, jnp.where(in_x1 | in_x2, c, 0.0)))
            sin_t.append(jnp.where(in_x1, -s, jnp.where(in_x2, s, 0.0)))
        cos_t = jnp.concatenate(cos_t, axis=0)
        sin_t = jnp.concatenate(sin_t, axis=0)
        yield

        def rope(z):
            return z * cos_t + pltpu.roll(z, LANES - QK_ROPE, 1) * sin_t

        k_rope = rope(k_rope)
        for h in range(N_HEADS):
            sl = slice(h * HEAD_PAD, (h + 1) * HEAD_PAD)
            q_ref[rows, sl] = (rope(q_all[:, sl]) * Q_SCALE).astype(bf)
            k_ref[rows, sl] = (k_all[:, sl] + k_rope).astype(bf)
            if h % 2:
                yield

    chains = [rows_chain(part) for part in range(QKV_SPLIT)]
    while chains:
        chains = [c for c in chains if next(c, True) is None]


def _qkv_call(x, pos_packed, invf, g, win, gq, wuq, gkv, wuk, wuvt):
    s = x.shape[0]
    tm = V_CHUNK
    row = lambda i: (i, 0)
    fixed = lambda i: (0, 0)
    full = lambda arr: pl.BlockSpec(arr.shape, fixed)
    qk = jax.ShapeDtypeStruct((s, N_HEADS * HEAD_PAD), jnp.bfloat16)
    vt = jax.ShapeDtypeStruct((s // tm, N_HEADS * V_ROWS, tm), jnp.bfloat16)
    return pl.pallas_call(
        _qkv_kernel,
        grid=(s // tm,),
        in_specs=[pl.BlockSpec((tm, D_MODEL), row), pl.BlockSpec((tm // ROPE_PACK, LANES), row),
                  full(invf), full(g), full(win), full(gq), full(wuq), full(gkv), full(wuk),
                  full(wuvt)],
        out_specs=[pl.BlockSpec((tm, N_HEADS * HEAD_PAD), row),
                   pl.BlockSpec((tm, N_HEADS * HEAD_PAD), row),
                   pl.BlockSpec((1, N_HEADS * V_ROWS, tm), lambda i: (i, 0, 0))],
        out_shape=[qk, qk, vt],
        compiler_params=pltpu.CompilerParams(
            dimension_semantics=("arbitrary",), vmem_limit_bytes=VMEM_LIMIT,
            allow_input_fusion=[i in (4, 6, 8, 9) for i in range(10)]),
        name="qkv_proj",
    )(x, pos_packed, invf, g, win, gq, wuq, gkv, wuk, wuvt)


class _Unit(NamedTuple):
    head: int
    col: int
    kv_start: Any
    width: int
    shift: Optional[int]


def _attn_kernel(q_ref, k_ref, vt_ref, o_ref, m_ref, acc_ref, *s_bufs, tk):
    tq = q_ref.shape[0]
    n_buf = len(s_bufs)
    ahead = n_buf - 1
    i = pl.program_id(1)
    m_ref[...] = jnp.full(m_ref.shape, MASK_VALUE, jnp.float32)
    acc_ref[...] = jnp.zeros(acc_ref.shape, jnp.float32)

    def scores(u, slot):
        sl = slice(u.head * HEAD_PAD, (u.head + 1) * HEAD_PAD)
        s = lax.dot_general(k_ref[pl.ds(u.kv_start, u.width), sl], q_ref[u.col:u.col + ATT_TILE, sl],
                            (((1,), (1,)), ((), ())), preferred_element_type=jnp.float32)
        if u.shift is not None:
            r = lax.broadcasted_iota(jnp.int32, s.shape, 0)
            c = lax.broadcasted_iota(jnp.int32, s.shape, 1)
            s = jnp.where(r <= c + u.shift, s, MASK_VALUE)
        s_bufs[slot][:u.width, :] = s
        rows = u.width // MAX_CHAINS
        parts = [jnp.max(s[g * rows:(g + 1) * rows], axis=0, keepdims=True)
                 for g in range(MAX_CHAINS)]
        while len(parts) > 1:
            parts = [jnp.maximum(a, b) for a, b in zip(parts[::2], parts[1::2])]
        return parts[0]

    def update(u, slot, col_max):
        qc = slice(u.col, u.col + ATT_TILE)
        m_prev = m_ref[u.head, :, qc]
        m_new = jnp.maximum(m_prev, col_max)
        alpha = jnp.exp2(m_prev - m_new)
        p = jnp.exp2(s_bufs[slot][:u.width, :] - m_new).astype(jnp.bfloat16)
        vrows = slice(u.head * V_ROWS, (u.head + 1) * V_ROWS)
        slab0 = u.kv_start // V_CHUNK
        pv = None
        for c, lo in enumerate(range(0, u.width, V_CHUNK)):
            n = min(V_CHUNK, u.width - lo)
            part = _dot(vt_ref[slab0 + c, vrows, :n], p[lo:lo + n])
            pv = part if pv is None else pv + part
        acc_ref[u.head, :, qc] = alpha * acc_ref[u.head, :, qc] + pv
        m_ref[u.head, :, qc] = m_new

    def run(units, ready=(), then=(), after=None):
        seq = list(units) + list(then)
        maxes = dict(enumerate(ready))
        for n in range(len(maxes), min(ahead, len(seq))):
            maxes[n] = scores(seq[n], n % n_buf)
        for n, u in enumerate(units):
            if n + ahead < len(seq):
                maxes[n + ahead] = scores(seq[n + ahead], (n + ahead) % n_buf)
            update(u, n % n_buf, maxes.pop(n))
            if after is not None:
                after(u)
        return tuple(maxes[len(units) + n] for n in range(len(then)))

    tile_order = list(range(tk, tq, ATT_TILE)) + list(range(0, tk, ATT_TILE))

    def full_units(block):
        kv_start = pl.multiple_of(block * tk, tk)
        return [_Unit(hh, c0, kv_start, tk, None) for hh in range(2) for c0 in tile_order]

    n_main = (i * tq) // tk
    assert tq == 2 * tk and ahead <= tk // ATT_TILE and (2 * (tq // ATT_TILE)) % n_buf == 0

    def body(j, first_maxes):
        return run(full_units(j), ready=first_maxes, then=full_units(j + 1)[:ahead])

    first = full_units(0)[:ahead]
    first_maxes = lax.fori_loop(0, n_main, body,
                                tuple(scores(u, n) for n, u in enumerate(first)))

    diag_start = pl.multiple_of(i * tq, tq)
    quarter = [u for u in full_units(n_main) if u.col >= tk]
    masked = []
    for c0 in range(tk - ATT_TILE, -ATT_TILE, -ATT_TILE):
        for hh in range(2):
            masked.append(_Unit(hh, c0, diag_start, c0 + ATT_TILE, c0))
        for hh in range(2):
            masked.append(_Unit(hh, c0 + tk, diag_start + tk, c0 + ATT_TILE, c0))

    def finish(u):
        if u.head != 1 or u.shift is None:
            return
        qc = slice(u.col, u.col + ATT_TILE)
        o_t = jnp.concatenate(
            [acc_ref[hh, :V_HEAD, qc] / acc_ref[hh, V_HEAD:V_HEAD + 1, qc] for hh in range(2)],
            axis=0)
        o_ref[qc, :] = o_t.T.astype(o_ref.dtype)

    run(quarter + masked, ready=first_maxes, after=finish)


def _attn_call(q, k, vt):
    s = q.shape[0]
    tq = ATT_Q
    pair = 2 * HEAD_PAD
    return pl.pallas_call(
        functools.partial(_attn_kernel, tk=ATT_K),
        grid=(N_HEADS // 2, s // tq),
        in_specs=[pl.BlockSpec((tq, pair), lambda hp, i: (i, hp)),
                  pl.BlockSpec((s, pair), lambda hp, i: (0, hp)),
                  pl.BlockSpec((s // V_CHUNK, 2 * V_ROWS, V_CHUNK), lambda hp, i: (0, hp, 0))],
        out_specs=pl.BlockSpec((tq, 2 * V_HEAD), lambda hp, i: (i, hp)),
        out_shape=jax.ShapeDtypeStruct((s, N_HEADS * V_HEAD), jnp.bfloat16),
        scratch_shapes=[pltpu.VMEM((2, 1, tq), jnp.float32),
                        pltpu.VMEM((2, V_ROWS, tq), jnp.float32),
                        ] + [pltpu.VMEM((ATT_K, ATT_TILE), jnp.float32)] * S_BUFS,
        compiler_params=pltpu.CompilerParams(
            dimension_semantics=("arbitrary", "arbitrary"), vmem_limit_bytes=VMEM_LIMIT),
        name="mla_attention",
    )(q, k, vt)


def _post_kernel(x_ref, halo_ref, attn_ref, p_ref, g_pre_ref, wpool_in_ref, wgate_ref, bgate_ref,
                 wpool_ref, pscale_ref, wba_ref, wbp_ref, wout_ref, g_post_ref, g_mlp_ref,
                 wff1_ref, wff2_ref, g_pmlp_ref, wpe_ref, wpg_ref, g_ple_ref, o_ref, ext_ref):
    bf = jnp.bfloat16
    tm = x_ref.shape[0]
    ts = tm // POST_SPLIT
    i = pl.program_id(0)

    def rows_chain(part):
        r0 = part * ts
        rows = slice(r0, r0 + ts)
        x = x_ref[rows, :]
        a = _rms(x, g_pre_ref[...]).astype(bf)
        yield

        if part == 0:
            x_halo, have_halo = halo_ref[...], i > 0
        else:
            x_halo, have_halo = x_ref[r0 - POOL_HALO:r0, :], True
        u_halo = _dot(_rms(x_halo, g_pre_ref[...]).astype(bf), wpool_in_ref[...])
        u = _dot(a, wpool_in_ref[...])
        ext = ext_ref.at[part]
        ext[0:POOL_HALO, :] = jnp.where(have_halo, u_halo, 0.0)
        ext[POOL_HALO:, :] = u
        yield
        gates = jax.nn.sigmoid(_dot(a, wgate_ref[...]) + bgate_ref[...])
        attn_branch = gates[:, :D_MODEL] * _dot(attn_ref[rows, :], wba_ref[...])
        yield
        t = i * tm + r0 + lax.broadcasted_iota(jnp.int32, (ts, 1), 0)
        pooled = []
        for g, w in enumerate(POOL_WINDOWS):
            cols = slice(g * POOL_GROUP, (g + 1) * POOL_GROUP)
            wsum = ext[:, cols]
            shift = 1
            while shift < w:
                wsum = wsum + pltpu.roll(wsum, shift, 0)
                shift *= 2
            wsum = wsum[POOL_HALO:, :]
            cnt = jnp.minimum(t + 1, w).astype(jnp.float32)
            d = wsum / cnt - u[:, cols]
            pooled.append(_dot(d.astype(bf), wpool_ref[g]))
        pooled = jnp.concatenate(pooled, axis=1) * pscale_ref[...]
        yield

        merged = attn_branch + gates[:, D_MODEL:] * _dot(pooled.astype(bf), wbp_ref[...])
        yield
        y = _dot(merged.astype(bf), wout_ref[...])
        yield
        h = x + _rms(y, g_post_ref[...])

        m = _rms(h, g_mlp_ref[...]).astype(bf)
        yield
        f = jnp.zeros((ts, D_MODEL), jnp.float32)
        for c in range(D_FF // FF_CHUNK):
            cols = slice(c * FF_CHUNK, (c + 1) * FF_CHUNK)
            hid = jnp.square(jnp.maximum(_dot(m, wff1_ref[:, cols]), 0.0))
            f = f + _dot(hid.astype(bf), wff2_ref[cols, :])
            yield
        h = h + _rms(f, g_pmlp_ref[...])
        yield

        e = _dot(p_ref[rows, :].astype(bf), wpe_ref[...])
        pg = jax.nn.sigmoid(_dot(h.astype(bf), wpg_ref[...]))
        yield
        o_ref[rows, :] = h + _rms(pg * e, g_ple_ref[...])

    chains = [rows_chain(part) for part in range(POST_SPLIT)]
    while chains:
        chains = [c for c in chains if next(c, True) is None]


def _post_call(x, attn, p, g_pre, wpool_in, wgate, bgate, wpool, pscale, wba, wbp, wout,
               g_post, g_mlp, wff1, wff2, g_pmlp, wpe, wpg, g_ple):
    s = x.shape[0]
    tm = POST_ROWS
    row = lambda i: (i, 0)
    halo_blocks = tm // POOL_HALO

    def const(arr):
        zeros = (0,) * arr.ndim
        return pl.BlockSpec(arr.shape, lambda i: zeros, pipeline_mode=pl.Buffered(1))

    weights = (g_pre, wpool_in, wgate, bgate, wpool, pscale, wba, wbp, wout, g_post, g_mlp,
               wff1, wff2, g_pmlp, wpe, wpg, g_ple)
    return pl.pallas_call(
        _post_kernel,
        grid=(s // tm,),
        in_specs=[pl.BlockSpec((tm, D_MODEL), row),
                  pl.BlockSpec((POOL_HALO, D_MODEL),
                               lambda i: (jnp.maximum(i * halo_blocks - 1, 0), 0)),
                  pl.BlockSpec((tm, N_HEADS * V_HEAD), row),
                  pl.BlockSpec((tm, PLE_DIM), row)] + [const(w) for w in weights],
        out_specs=pl.BlockSpec((tm, D_MODEL), row),
        out_shape=jax.ShapeDtypeStruct((s, D_MODEL), jnp.float32),
        scratch_shapes=[pltpu.VMEM((POST_SPLIT, tm // POST_SPLIT + POOL_HALO, POOL_WIDTH),
                                   jnp.float32)],
        compiler_params=pltpu.CompilerParams(
            dimension_semantics=("arbitrary",), vmem_limit_bytes=VMEM_LIMIT,
            allow_input_fusion=[i in (5, 6, 8, 10, 11, 12, 15, 16, 18, 19)
                                for i in range(4 + len(weights))]),
        name="post_attention",
    )(x, x, attn, p, *weights)


def _pad_heads(w, per_head):
    kdim = w.shape[0]
    w = w.reshape(kdim, N_HEADS, per_head)
    w = jnp.pad(w, ((0, 0), (0, 0), (0, HEAD_PAD - per_head)))
    return w.reshape(kdim, N_HEADS * HEAD_PAD)


def _rope_tile(nope, rope):
    x1, x2 = rope[..., :ROPE_HALF], rope[..., ROPE_HALF:]
    return jnp.concatenate([nope, x1, x2, x2, x1], axis=-1)


def _layer(h, p, pos, invf, g_pre_mix, w_in, b_gate, g_q, w_uq, g_kv, w_ukv, w_pool, pool_scale,
           w_branch_attn, w_branch_pool, w_out, g_post_mix, g_pre_mlp, w_ff1, w_ff2, g_post_mlp,
           w_ple_proj, w_ple_gate, g_ple):
    bf = jnp.bfloat16
    row = lambda v: v.reshape(1, -1)
    o_kv = Q_LORA
    o_kr = o_kv + KV_LORA
    o_pool = o_kr + QK_ROPE
    o_gate = o_pool + POOL_WIDTH

    w_kr = _rope_tile(jnp.zeros((D_MODEL, QK_NOPE), w_in.dtype), w_in[:, o_kr:o_pool])
    w_in_qkv = jnp.concatenate([w_in[:, :o_kr], w_kr], axis=1).astype(bf)
    w_uq_h = w_uq.reshape(Q_LORA, N_HEADS, QK_NOPE + QK_ROPE)
    w_uq_p = _rope_tile(w_uq_h[..., :QK_NOPE], w_uq_h[..., QK_NOPE:]).reshape(Q_LORA, -1).astype(bf)
    w_ukv_h = w_ukv.reshape(KV_LORA, N_HEADS, QK_NOPE + V_HEAD)
    w_uk_p = _pad_heads(w_ukv_h[:, :, :QK_NOPE].reshape(KV_LORA, -1), QK_NOPE).astype(bf)
    w_uv_t = jnp.pad(jnp.transpose(w_ukv_h[:, :, QK_NOPE:], (1, 2, 0)),
                     ((0, 0), (0, V_ROWS - V_HEAD), (0, 0))).reshape(N_HEADS * V_ROWS, KV_LORA).astype(bf)

    q, k, vt = _qkv_call(h, pos, invf, row(g_pre_mix), w_in_qkv, row(g_q), w_uq_p, row(g_kv),
                         w_uk_p, w_uv_t)
    attn = _attn_call(q, k, vt)
    return _post_call(
        h, attn, p, row(g_pre_mix), w_in[:, o_pool:o_gate].astype(bf), w_in[:, o_gate:].astype(bf),
        row(b_gate), w_pool.astype(bf), row(pool_scale), w_branch_attn.astype(bf),
        w_branch_pool.astype(bf), w_out.astype(bf), row(g_post_mix), row(g_pre_mlp),
        w_ff1.astype(bf), w_ff2.astype(bf), row(g_post_mlp), w_ple_proj.astype(bf),
        w_ple_gate.astype(bf), row(g_ple))


@jax.jit
def kernel(x, p, positions, g_pre_mix, w_in, b_gate, g_q, w_uq, g_kv, w_ukv, w_pool, pool_scale,
           w_branch_attn, w_branch_pool, w_out, g_post_mix, g_pre_mlp, w_ff1, w_ff2, g_post_mlp,
           w_ple_proj, w_ple_gate, g_ple):
    batch, seq, _ = x.shape
    depth = w_in.shape[0]
    inv_freq = ROPE_THETA ** (-jnp.arange(0, QK_ROPE, 2, dtype=jnp.float32) / QK_ROPE)
    invf = jnp.tile(inv_freq, 2 * ROPE_PACK).reshape(1, LANES)
    ts = V_CHUNK // QKV_SPLIT
    outs = []
    for b in range(batch):
        h = x[b]
        pos = positions[b].reshape(seq // ts, ROPE_PACK, ts // ROPE_PACK).transpose(0, 2, 1)
        pos = jnp.repeat(pos.reshape(seq // ROPE_PACK, ROPE_PACK), QK_ROPE, axis=1)
        for l in range(depth):
            h = _layer(h, p[l, b], pos, invf, g_pre_mix[l], w_in[l], b_gate[l], g_q[l], w_uq[l],
                       g_kv[l], w_ukv[l], w_pool[l], pool_scale[l], w_branch_attn[l],
                       w_branch_pool[l], w_out[l], g_post_mix[l], g_pre_mlp[l], w_ff1[l], w_ff2[l],
                       g_post_mlp[l], w_ple_proj[l], w_ple_gate[l], g_ple[l])
        outs.append(h)
    return jnp.stack(outs, axis=0)
```
